```python
import jax, jax.numpy as jnp
from jax import lax
import numpy as np

D_MODEL = 2048
BATCH = 16
SEQ = 256
DEPTH = 2
DEC_BATCH = 4
DEC_SEQ = 1024
PAST_LEN = 512

GRID_W = 64
CONV_WIDTH = 31
CONV_CH = D_MODEL // 2
N_HEADS_B = 8
HEAD_DK = 128
HEAD_DV = 128
KEY_DIM = N_HEADS_B * HEAD_DK
VAL_DIM = N_HEADS_B * HEAD_DV
SHORT_CONV = 3
CHUNK = 64
EPS = 1e-6
IN_SPLITS = (CONV_CH, CONV_CH, CONV_CH, KEY_DIM, KEY_DIM, VAL_DIM, VAL_DIM,
             2 * N_HEADS_B, 2 * N_HEADS_B, D_MODEL, D_MODEL)
IN_WIDTH = 3 * CONV_CH + 2 * KEY_DIM + 2 * VAL_DIM + 4 * N_HEADS_B + 2 * D_MODEL

kernel_name = "hybrid_conformer_gdn_diffusion_step"


def _rmsnorm(x, g):
    xf = x.astype(jnp.float32)
    y = xf * lax.rsqrt(jnp.mean(xf * xf, axis=-1, keepdims=True) + EPS)
    return (y * g.astype(jnp.float32)).astype(x.dtype)


def _layernorm(x, g, b):
    xf = x.astype(jnp.float32)
    mu = jnp.mean(xf, axis=-1, keepdims=True)
    var = jnp.mean(jnp.square(xf - mu), axis=-1, keepdims=True)
    y = (xf - mu) * lax.rsqrt(var + EPS)
    return (y * g.astype(jnp.float32) + b.astype(jnp.float32)).astype(x.dtype)


def _l2norm(x):
    return x * lax.rsqrt(jnp.sum(x * x, axis=-1, keepdims=True) + EPS)


def _dwconv(x, w):
    k = w.shape[0]
    return lax.conv_general_dilated(
        x, w[:, None, :].astype(x.dtype), window_strides=(1,),
        padding=[(k // 2, k // 2)], dimension_numbers=('NWC', 'WIO', 'NWC'),
        feature_group_count=x.shape[-1])


def _axial_dwconv(u, w):
    b, n, ch = u.shape
    rows = n // GRID_W
    half = ch // 2
    grid = u.reshape(b, rows, GRID_W, ch)
    horiz = _dwconv(grid[..., :half].reshape(b * rows, GRID_W, half), w[:, :half])
    horiz = horiz.reshape(b, rows, GRID_W, half)
    vert = grid[..., half:].transpose(0, 2, 1, 3).reshape(b * GRID_W, rows, ch - half)
    vert = _dwconv(vert, w[:, half:]).reshape(b, GRID_W, rows, ch - half).transpose(0, 2, 1, 3)
    return jnp.concatenate([horiz, vert], axis=-1).reshape(b, n, ch)


def _chunk_gated_delta(q, k, v, g, beta, s0):
    b, n_tok, h, dk = q.shape
    dv = v.shape[-1]
    nc = n_tok // CHUNK

    def blk(t):
        t = jnp.moveaxis(t, 2, 1)
        return t.reshape(b, h, nc, CHUNK, *t.shape[3:])

    q, k, v, g, beta = blk(q), blk(k), blk(v), blk(g), blk(beta)
    g = jnp.cumsum(g, axis=-1)
    tri = jnp.tril(jnp.ones((CHUNK, CHUNK), dtype=bool))
    strict = jnp.tril(jnp.ones((CHUNK, CHUNK), dtype=bool), -1)
    decay = jnp.exp(jnp.where(tri, g[..., :, None] - g[..., None, :], -jnp.inf))
    kb = k * beta[..., None]
    a = jnp.einsum('bhnid,bhnjd->bhnij', kb, k) * decay
    a = jnp.where(strict, a, 0.0) + jnp.eye(CHUNK, dtype=a.dtype)
    u = lax.linalg.triangular_solve(a, v * beta[..., None], left_side=True, lower=True,
                                    unit_diagonal=True)
    w = lax.linalg.triangular_solve(a, kb * jnp.exp(g)[..., None], left_side=True,
                                    lower=True, unit_diagonal=True)
    qk = jnp.einsum('bhnid,bhnjd->bhnij', q, k) * decay
    q_dec = q * jnp.exp(g)[..., None]
    g_last = g[..., -1]
    k_dec = k * jnp.exp(g_last[..., None] - g)[..., None]

    def step(s, xs):
        u_c, w_c, qk_c, qd_c, kd_c, gl_c = xs
        v_new = u_c - jnp.einsum('bhcd,bhde->bhce', w_c, s)
        o_c = jnp.einsum('bhcd,bhde->bhce', qd_c, s) + jnp.einsum('bhij,bhje->bhie', qk_c, v_new)
        s = s * jnp.exp(gl_c)[..., None, None] + jnp.einsum('bhcd,bhce->bhde', kd_c, v_new)
        return s, o_c

    xs = tuple(jnp.moveaxis(t, 2, 0) for t in (u, w, qk, q_dec, k_dec, g_last))
    s_fin, o = lax.scan(step, s0, xs)
    o = o.transpose(1, 0, 3, 2, 4).reshape(b, n_tok, h, dv)
    return o, s_fin


def _layer(x, cvec, s0, latent, w_mod, b_mod, norm_g, w_in, conv_a_w, conv_a_b, ln_a_g,
           ln_a_b, w_pa, conv_qkv_w, a_log, dt_bias, head_norm_g, w_pb, w_o):
    b, n, _ = x.shape
    mod = (jax.nn.silu(cvec) @ w_mod + b_mod)[:, None, :]
    shift, scale, gate = jnp.split(mod, 3, axis=-1)
    hn = _rmsnorm(x, norm_g) * (1.0 + scale) + shift
    proj = hn @ w_in
    idx = np.cumsum(IN_SPLITS)[:-1].tolist()
    (a_val, a_glu, a_z, q, k, v, z_b, beta_l, alpha_l, gate_a, gate_b) = jnp.split(proj, idx, axis=-1)

    ua = a_val * jax.nn.sigmoid(a_glu)
    ua = _axial_dwconv(ua, conv_a_w) if latent else _dwconv(ua, conv_a_w)
    ua = jax.nn.silu(_layernorm(ua + conv_a_b, ln_a_g, ln_a_b))
    out_a = (ua * jax.nn.silu(a_z)) @ w_pa

    qkv = jax.nn.silu(_dwconv(jnp.concatenate([q, k, v], axis=-1), conv_qkv_w))
    qkv = qkv.astype(jnp.float32)
    qf, kf, vf = jnp.split(qkv, [KEY_DIM, 2 * KEY_DIM], axis=-1)
    qf = _l2norm(qf.reshape(b, n, N_HEADS_B, HEAD_DK)) * (HEAD_DK ** -0.5)
    kf = _l2norm(kf.reshape(b, n, N_HEADS_B, HEAD_DK))
    vf = vf.reshape(b, n, N_HEADS_B, HEAD_DV)
    beta = jax.nn.sigmoid(beta_l.astype(jnp.float32)).reshape(b, n, 2, N_HEADS_B)
    g = -jnp.exp(a_log.astype(jnp.float32)) * jax.nn.softplus(
        alpha_l.astype(jnp.float32).reshape(b, n, 2, N_HEADS_B) + dt_bias.astype(jnp.float32))
    o_f, s_f = _chunk_gated_delta(qf, kf, vf, g[:, :, 0], beta[:, :, 0], s0[:, 0])
    flip = lambda t: jnp.flip(t, axis=1)
    o_b, s_b = _chunk_gated_delta(flip(qf), flip(kf), flip(vf), flip(g[:, :, 1]),
                                  flip(beta[:, :, 1]), s0[:, 1])
    o = (o_f + flip(o_b)).astype(x.dtype)
    o = _rmsnorm(o, head_norm_g).reshape(b, n, VAL_DIM)
    out_b = (o * jax.nn.silu(z_b)) @ w_pb

    merged = jax.nn.sigmoid(gate_a) * out_a + jax.nn.sigmoid(gate_b) * out_b
    x = x + gate * (merged @ w_o)
    return x, jnp.stack([s_f, s_b], axis=1)


def setup_inputs(seed: int = 0) -> dict:
    key = jax.random.key(seed)
    ks = jax.random.split(key, 24)
    f32 = jnp.float32
    nrm = lambda k, shape, s: jax.random.normal(k, shape, f32) * s
    dt = jnp.exp(jax.random.uniform(ks[15], (DEPTH, 2, N_HEADS_B), f32,
                                    np.log(1e-3), np.log(1e-1)))
    return {
        "x_prompt": nrm(ks[0], (BATCH, SEQ, D_MODEL), 1.0),
        "x_sample": nrm(ks[1], (DEC_BATCH, DEC_SEQ, D_MODEL), 1.0),
        "state_delta": nrm(ks[2], (DEC_BATCH, DEPTH, 2, N_HEADS_B, HEAD_DK, HEAD_DV), 0.5),
        "c": nrm(ks[3], (DEC_BATCH, D_MODEL), 1.0),
        "c_ctx": nrm(ks[4], (D_MODEL,), 1.0),
        "w_mod": nrm(ks[5], (DEPTH, D_MODEL, 3 * D_MODEL), 0.5 * D_MODEL ** -0.5),
        "b_mod": nrm(ks[6], (DEPTH, 3 * D_MODEL), 0.01),
        "norm_g": 1.0 + nrm(ks[7], (DEPTH, D_MODEL), 0.01),
        "w_in": nrm(ks[8], (DEPTH, D_MODEL, IN_WIDTH), D_MODEL ** -0.5),
        "conv_a_w": nrm(ks[9], (DEPTH, CONV_WIDTH, CONV_CH), CONV_WIDTH ** -0.5),
        "conv_a_b": nrm(ks[10], (DEPTH, CONV_CH), 0.01),
        "ln_a_g": 1.0 + nrm(ks[11], (DEPTH, CONV_CH), 0.01),
        "ln_a_b": nrm(ks[12], (DEPTH, CONV_CH), 0.01),
        "w_pa": nrm(ks[13], (DEPTH, CONV_CH, D_MODEL), CONV_CH ** -0.5),
        "conv_qkv_w": nrm(ks[14], (DEPTH, SHORT_CONV, 2 * KEY_DIM + VAL_DIM), SHORT_CONV ** -0.5),
        "a_log": jnp.log(jax.random.uniform(ks[16], (DEPTH, 2, N_HEADS_B), f32, 1.0, 16.0)),
        "dt_bias": dt + jnp.log(-jnp.expm1(-dt)),
        "head_norm_g": 1.0 + nrm(ks[17], (DEPTH, HEAD_DV), 0.01),
        "w_pb": nrm(ks[18], (DEPTH, VAL_DIM, D_MODEL), VAL_DIM ** -0.5),
        "w_o": nrm(ks[19], (DEPTH, D_MODEL, D_MODEL), D_MODEL ** -0.5),
        "final_norm_g": 1.0 + nrm(ks[20], (D_MODEL,), 0.01),
    }


def reference(x_prompt, x_sample, state_delta, c, c_ctx, w_mod, b_mod, norm_g, w_in,
              conv_a_w, conv_a_b, ln_a_g, ln_a_b, w_pa, conv_qkv_w, a_log, dt_bias,
              head_norm_g, w_pb, w_o, final_norm_g):
    def layer_params(l):
        return (w_mod[l], b_mod[l], norm_g[l], w_in[l], conv_a_w[l], conv_a_b[l], ln_a_g[l],
                ln_a_b[l], w_pa[l], conv_qkv_w[l], a_log[l], dt_bias[l], head_norm_g[l],
                w_pb[l], w_o[l])

    zero_state = jnp.zeros((x_prompt.shape[0], 2, N_HEADS_B, HEAD_DK, HEAD_DV), jnp.float32)
    cvec_ctx = c_ctx[None, :]
    h = x_prompt
    ctx_states = []
    for l in range(DEPTH):
        h, st = _layer(h, cvec_ctx, zero_state, False, *layer_params(l))
        ctx_states.append(st)
    y_prompt = _rmsnorm(h, final_norm_g)
    state_delta_new = jnp.stack(ctx_states, axis=1).astype(x_prompt.dtype)

    hs = x_sample
    for l in range(DEPTH):
        hs, _ = _layer(hs, c, state_delta[:, l].astype(jnp.float32), True, *layer_params(l))
    y_sample = _rmsnorm(hs, final_norm_g)
    return (y_prompt, y_sample, state_delta_new)
```

```python
import functools

import jax
import jax.numpy as jnp
from jax import lax
from jax.experimental import pallas as pl
from jax.experimental.pallas import tpu as pltpu

F32 = jnp.float32
BF16 = jnp.bfloat16

EPS = 1e-6
GRID_W = 64
CHUNK = 64
N_HEADS = 8
HEAD_DIM = 128
LANES = 128
MOD_ROWS = 8
V7X_VMEM_LIMIT_BYTES = 56 * 1024 * 1024


def _sigmoid(x):
    return jax.nn.sigmoid(x)


def _silu(x):
    return x * jax.nn.sigmoid(x)


def _params(semantics):
    return pltpu.CompilerParams(dimension_semantics=semantics,
                                vmem_limit_bytes=V7X_VMEM_LIMIT_BYTES)


def _dot(a, b):
    return jnp.dot(a.astype(BF16), b.astype(BF16), preferred_element_type=F32)


def _dot_nt(a, b):
    return lax.dot_general(a.astype(BF16), b.astype(BF16), (((1,), (1,)), ((), ())),
                           preferred_element_type=F32)


def _dot_tn(a, b):
    return lax.dot_general(a.astype(BF16), b.astype(BF16), (((0,), (0,)), ((), ())),
                           preferred_element_type=F32)


def _mod_kernel(cv_ref, w_ref, b_ref, o_ref):
    a = _silu(cv_ref[...])
    o_ref[...] = _dot(a, w_ref[...]) + b_ref[...]


def _mod_call(cv, w_mod, b_mod):
    depth, d, d3 = w_mod.shape
    tn = 1024
    return pl.pallas_call(
        _mod_kernel,
        grid=(depth, d3 // tn),
        in_specs=[pl.BlockSpec((MOD_ROWS, d), lambda l, j: (0, 0)),
                  pl.BlockSpec((None, d, tn), lambda l, j: (l, 0, j)),
                  pl.BlockSpec((None, 1, tn), lambda l, j: (l, 0, j))],
        out_specs=pl.BlockSpec((None, MOD_ROWS, tn), lambda l, j: (l, 0, j)),
        out_shape=jax.ShapeDtypeStruct((depth, MOD_ROWS, d3), F32),
        compiler_params=_params(("parallel", "parallel")),
        name="mod",
    )(cv, w_mod, b_mod.reshape(depth, 1, d3))


def _mod_row(i, tm, tokens_per_mod):
    if tokens_per_mod is None:
        return 0
    return 1 + (i * tm) // tokens_per_mod


def _in_proj_kernel(x_ref, mod_ref, ng_ref, w_ref, wba_ref, alog_ref, dtb_ref,
                    proj_ref, ba_ref, hn_ref, *, tm, d):
    j = pl.program_id(1)

    @pl.when(j == 0)
    def _():
        shift = mod_ref[:, 0:d]
        scale = mod_ref[:, d:2 * d]
        gain = ng_ref[...] * (1.0 + scale)
        rb = 64

        def norm_rows(t, carry):
            rows = pl.ds(pl.multiple_of(t * rb, rb), rb)
            x = x_ref[rows, :]
            inv = lax.rsqrt(jnp.mean(x * x, axis=-1, keepdims=True) + EPS)
            hn_ref[rows, :] = (x * inv * gain + shift).astype(BF16)
            return carry

        lax.fori_loop(0, tm // rb, norm_rows, 0)

        pba = jnp.dot(hn_ref[...], wba_ref[...], preferred_element_type=F32)
        beta = _sigmoid(pba)
        z = pba + dtb_ref[...]
        softplus = jnp.maximum(z, 0.0) + jnp.log(1.0 + jnp.exp(-jnp.abs(z)))
        g = -jnp.exp(alog_ref[...]) * softplus
        pos = lax.broadcasted_iota(jnp.int32, (tm, LANES), 0) & (CHUNK - 1)
        pre = g
        suf = g
        s = 1
        while s < CHUNK:
            pre = pre + jnp.where(pos >= s, pltpu.roll(pre, s, 0), 0.0)
            suf = suf + jnp.where(pos < CHUNK - s, pltpu.roll(suf, tm - s, 0), 0.0)
            s *= 2
        lane = lax.broadcasted_iota(jnp.int32, (tm, LANES), 1)
        ba_ref[...] = jnp.where(lane < 2 * N_HEADS, beta,
                                jnp.where(lane < 3 * N_HEADS, pre, suf))

    proj_ref[...] = jnp.dot(hn_ref[...], w_ref[...],
                            preferred_element_type=F32).astype(proj_ref.dtype)


def _in_proj_call(x, mod, norm_g, w_main, w_ba, alog_row, dtb_row, tokens_per_mod):
    n, d = x.shape
    width = w_main.shape[1]
    tm, tn = 512, 512
    kern = functools.partial(_in_proj_kernel, tm=tm, d=d)
    return pl.pallas_call(
        kern,
        grid=(n // tm, width // tn),
        in_specs=[pl.BlockSpec((tm, d), lambda i, j: (i, 0)),
                  pl.BlockSpec((None, 1, 3 * d), lambda i, j: (_mod_row(i, tm, tokens_per_mod), 0, 0)),
                  pl.BlockSpec((1, d), lambda i, j: (0, 0)),
                  pl.BlockSpec((d, tn), lambda i, j: (0, j)),
                  pl.BlockSpec((d, LANES), lambda i, j: (0, 0)),
                  pl.BlockSpec((1, LANES), lambda i, j: (0, 0)),
                  pl.BlockSpec((1, LANES), lambda i, j: (0, 0))],
        out_specs=[pl.BlockSpec((tm, tn), lambda i, j: (i, j)),
                   pl.BlockSpec((tm, LANES), lambda i, j: (i, 0))],
        out_shape=[jax.ShapeDtypeStruct((n, width), BF16),
                   jax.ShapeDtypeStruct((n, LANES), F32)],
        scratch_shapes=[pltpu.VMEM((tm, d), BF16)],
        compiler_params=_params(("parallel", "arbitrary")),
        name="in_proj",
    )(x, mod, norm_g, w_main, w_ba, alog_row, dtb_row)


def _ln_swish_gate(conv_ref, az_ref, g_ref, b_ref, o_ref, n_rows):
    rb = 32
    gain = g_ref[...]
    bias = b_ref[...]

    def rows_body(t, carry):
        rows = pl.ds(pl.multiple_of(t * rb, rb), rb)
        z = conv_ref[rows, :]
        mu = jnp.mean(z, axis=-1, keepdims=True)
        zc = z - mu
        var = jnp.mean(zc * zc, axis=-1, keepdims=True)
        y = _silu(zc * lax.rsqrt(var + EPS) * gain + bias)
        o_ref[rows, :] = (y * _silu(az_ref[rows, :].astype(F32))).astype(o_ref.dtype)
        return carry

    lax.fori_loop(0, n_rows // rb, rows_body, 0)


def _conv_seq_kernel(av_ref, ag_ref, az_ref, w_ref, cb_ref, g_ref, b_ref, o_ref,
                     pad_ref, conv_ref, *, seq, taps):
    half = taps // 2
    lead = 16
    ch = av_ref.shape[1]
    pad_ref[0:lead, :] = jnp.zeros((lead, ch), F32)
    pad_ref[lead + seq:lead + seq + lead, :] = jnp.zeros((lead, ch), F32)
    pad_ref[lead:lead + seq, :] = av_ref[...].astype(F32) * _sigmoid(ag_ref[...].astype(F32))
    rb = 128

    def lane_block(c, carry):
        lanes = pl.ds(pl.multiple_of(c * LANES, LANES), LANES)
        for r in range(seq // rb):
            acc = jnp.zeros((rb, LANES), F32)
            for j in range(taps):
                acc = acc + w_ref[j:j + 1, lanes] * pad_ref[pl.ds(r * rb + lead - half + j, rb), lanes]
            conv_ref[r * rb:(r + 1) * rb, lanes] = acc + cb_ref[:, lanes]
        return carry

    lax.fori_loop(0, ch // LANES, lane_block, 0)
    _ln_swish_gate(conv_ref, az_ref, g_ref, b_ref, o_ref, seq)


def _conv_axial_kernel(av_ref, ag_ref, az_ref, w_ref, cb_ref, g_ref, b_ref, o_ref,
                       pad_ref, ver_ref, conv_ref, *, grid_h, taps):
    half = taps // 2
    lead = 16
    ch = av_ref.shape[1]
    hc = ch // 2
    n_tok = grid_h * GRID_W
    for r in range(grid_h):
        rows = slice(r * GRID_W, (r + 1) * GRID_W)
        ua = av_ref[rows, :].astype(F32) * _sigmoid(ag_ref[rows, :].astype(F32))
        pad_ref[r, 0:lead, :] = jnp.zeros((lead, hc), F32)
        pad_ref[r, lead + GRID_W:lead + GRID_W + lead, :] = jnp.zeros((lead, hc), F32)
        pad_ref[r, lead:lead + GRID_W, :] = ua[:, 0:hc]
        ver_ref[rows, :] = ua[:, hc:ch]

    def lane_block(c, carry):
        lanes = pl.ds(pl.multiple_of(c * LANES, LANES), LANES)
        lanes_v = pl.ds(pl.multiple_of(hc + c * LANES, LANES), LANES)

        def grid_row(r, carry2):
            out_rows = pl.ds(pl.multiple_of(r * GRID_W, GRID_W), GRID_W)
            acc = jnp.zeros((GRID_W, LANES), F32)
            for j in range(taps):
                acc = acc + w_ref[j:j + 1, lanes] * pad_ref[r, pl.ds(lead - half + j, GRID_W), lanes]
            conv_ref[out_rows, lanes] = acc + cb_ref[:, lanes]
            return carry2

        lax.fori_loop(0, grid_h, grid_row, 0)
        for r in range(grid_h):
            acc = jnp.zeros((GRID_W, LANES), F32)
            for src in range(grid_h):
                tap = src + half - r
                acc = acc + (w_ref[tap:tap + 1, lanes_v]
                             * ver_ref[src * GRID_W:(src + 1) * GRID_W, lanes])
            conv_ref[r * GRID_W:(r + 1) * GRID_W, lanes_v] = acc + cb_ref[:, lanes_v]
        return carry

    lax.fori_loop(0, hc // LANES, lane_block, 0)
    _ln_swish_gate(conv_ref, az_ref, g_ref, b_ref, o_ref, n_tok)


def _conv_call(proj, conv_w, conv_b, ln_g, ln_b, batch, seq, col0, axial):
    taps, ch = conv_w.shape
    proj3 = proj.reshape(batch, seq, proj.shape[1])
    row = lambda v: v.reshape(1, ch)
    if axial:
        grid_h = seq // GRID_W
        assert taps // 2 >= grid_h - 1, "every grid row must lie inside the conv window"
        kern = functools.partial(_conv_axial_kernel, grid_h=grid_h, taps=taps)
        scratch = [pltpu.VMEM((grid_h, GRID_W + 32, ch // 2), F32),
                   pltpu.VMEM((seq, ch // 2), F32),
                   pltpu.VMEM((seq, ch), F32)]
    else:
        kern = functools.partial(_conv_seq_kernel, seq=seq, taps=taps)
        scratch = [pltpu.VMEM((seq + 32, ch), F32), pltpu.VMEM((seq, ch), F32)]
    tok = lambda k: pl.BlockSpec((None, seq, ch), lambda b: (b, 0, col0 + k))
    const = lambda shape: pl.BlockSpec(shape, lambda b: (0, 0))
    out = pl.pallas_call(
        kern,
        grid=(batch,),
        in_specs=[tok(0), tok(1), tok(2), const((taps, ch)), const((1, ch)), const((1, ch)),
                  const((1, ch))],
        out_specs=pl.BlockSpec((None, seq, ch), lambda b: (b, 0, 0)),
        out_shape=jax.ShapeDtypeStruct((batch, seq, ch), BF16),
        scratch_shapes=scratch,
        compiler_params=_params(("parallel",)),
        name="conv_axial" if axial else "conv_seq",
    )(proj3, proj3, proj3, conv_w, row(conv_b), row(ln_g), row(ln_b))
    return out.reshape(batch * seq, ch)


def _split_bf16(x):
    hi = x.astype(BF16)
    return hi, (x - hi.astype(F32)).astype(BF16)


def _dot_split(a, b):
    a_hi, a_lo = _split_bf16(a)
    b_hi, b_lo = _split_bf16(b)
    m = a.shape[0]
    by_hi = jnp.dot(jnp.concatenate([a_hi, a_lo], axis=0), b_hi, preferred_element_type=F32)
    return by_hi[0:m] + by_hi[m:2 * m] + jnp.dot(a_hi, b_lo, preferred_element_type=F32)


def _unit_lower_inverse(nmat, eye):
    p = eye - nmat
    s = _dot_split(nmat, nmat)
    power = 2
    while 2 * power < CHUNK:
        ps = _dot_split(jnp.concatenate([p, s], axis=0), s)
        p = p + ps[0:CHUNK]
        s = ps[CHUNK:2 * CHUNK]
        power *= 2
    return p + _dot_split(p, s)


def _delta_kernel(*refs, seq, hb_count, zero_init):
    if zero_init:
        (q_ref, k_ref, v_ref, z_ref, ba_ref, bat_ref, wq_ref, wk_ref, wv_ref, hng_ref,
         y_ref, sout_ref, qs, ks, vs, brep, grep, oacc, sscr) = refs
        s0_ref = None
    else:
        (q_ref, k_ref, v_ref, z_ref, ba_ref, bat_ref, wq_ref, wk_ref, wv_ref, hng_ref, s0_ref,
         y_ref, sout_ref, qs, ks, vs, brep, grep, oacc, sscr) = refs
    nc = seq // CHUNK
    h0 = pl.program_id(1) * hb_count
    row = lax.broadcasted_iota(jnp.int32, (seq, HEAD_DIM), 0)
    lane = lax.broadcasted_iota(jnp.int32, (seq, LANES), 1)

    def conv_swish(x_ref, w_ref, cols):
        x = x_ref[:, cols].astype(F32)
        prev = jnp.where(row >= 1, pltpu.roll(x, 1, 0), 0.0)
        nxt = jnp.where(row <= seq - 2, pltpu.roll(x, seq - 1, 0), 0.0)
        return _silu(w_ref[0:1, cols] * prev + w_ref[1:2, cols] * x + w_ref[2:3, cols] * nxt)

    def l2norm(x):
        return x * lax.rsqrt(jnp.sum(x * x, axis=-1, keepdims=True) + EPS)

    ba = ba_ref[...]
    for hb in range(hb_count):
        cols = slice(hb * HEAD_DIM, (hb + 1) * HEAD_DIM)
        qs[hb] = l2norm(conv_swish(q_ref, wq_ref, cols)) * (HEAD_DIM ** -0.5)
        ks[hb] = l2norm(conv_swish(k_ref, wk_ref, cols))
        vs[hb] = conv_swish(v_ref, wv_ref, cols)
        oacc[hb] = jnp.zeros((seq, HEAD_DIM), F32)
        for d in range(2):
            idx = d * N_HEADS + h0 + hb
            bcol = jnp.sum(jnp.where(lane == idx, ba, 0.0), axis=-1, keepdims=True)
            gcol = jnp.sum(jnp.where(lane == 2 * N_HEADS + idx, ba, 0.0), axis=-1, keepdims=True)
            brep[d, hb] = jnp.broadcast_to(bcol, (seq, LANES))
            grep[d, hb] = jnp.broadcast_to(gcol, (seq, LANES))
            if zero_init:
                sscr[d, hb] = jnp.zeros((HEAD_DIM, HEAD_DIM), F32)
            else:
                sscr[d, hb] = s0_ref[d, hb]

    ii = lax.broadcasted_iota(jnp.int32, (CHUNK, CHUNK), 0)
    jj = lax.broadcasted_iota(jnp.int32, (CHUNK, CHUNK), 1)
    eye = jnp.where(ii == jj, 1.0, 0.0).astype(F32)

    def chunk_step(c, carry):
        for hb in range(hb_count):
            for d in range(2):
                cidx = c if d == 0 else nc - 1 - c
                rows = pl.ds(pl.multiple_of(cidx * CHUNK, CHUNK), CHUNK)
                q = qs[hb, rows, :]
                k = ks[hb, rows, :]
                v = vs[hb, rows, :]
                b = brep[d, hb, rows, :]
                g = grep[d, hb, rows, :]
                grow = bat_ref[d * N_HEADS + h0 + hb, cidx]
                incl = (ii >= jj) if d == 0 else (ii <= jj)
                strict = (ii > jj) if d == 0 else (ii < jj)
                diff = g[:, 0:CHUNK] - grow
                decay = jnp.where(incl, jnp.exp(jnp.where(incl, diff, 0.0)), 0.0)
                kb = k * b
                nmat = jnp.where(strict, _dot_nt(kb, k) * decay, 0.0)
                tinv = _unit_lower_inverse(nmat, eye)
                eg = jnp.exp(g)
                uw = _dot(tinv, jnp.concatenate([v * b, kb * eg], axis=1))
                u = uw[:, 0:HEAD_DIM]
                w = uw[:, HEAD_DIM:2 * HEAD_DIM]
                qk = _dot_nt(q, k) * decay
                glast = g[CHUNK - 1:CHUNK, :] if d == 0 else g[0:1, :]
                k_dec = k * jnp.exp(glast - g)
                state = sscr[d, hb]
                ws = _dot(jnp.concatenate([w, q * eg], axis=0), state)
                v_new = u - ws[0:CHUNK]
                o = ws[CHUNK:2 * CHUNK] + _dot(qk, v_new)
                sscr[d, hb] = state * jnp.exp(glast) + _dot_tn(k_dec, v_new)
                oacc[hb, rows, :] = oacc[hb, rows, :] + o
        return carry

    lax.fori_loop(0, nc, chunk_step, 0)

    for hb in range(hb_count):
        cols = slice(hb * HEAD_DIM, (hb + 1) * HEAD_DIM)
        o = oacc[hb]
        o = o * lax.rsqrt(jnp.mean(o * o, axis=-1, keepdims=True) + EPS) * hng_ref[...]
        y_ref[:, cols] = (o * _silu(z_ref[:, cols].astype(F32))).astype(y_ref.dtype)
        for d in range(2):
            sout_ref[d, hb] = sscr[d, hb]


def _delta_call(proj, ba, conv_qkv_w, head_norm_g, s0, batch, seq, col_q):
    hb_count = 2
    n_hblk = N_HEADS // hb_count
    bw = hb_count * HEAD_DIM
    kd = N_HEADS * HEAD_DIM
    nc = seq // CHUNK
    proj3 = proj.reshape(batch, seq, proj.shape[1])
    ba3 = ba.reshape(batch, seq, LANES)
    bat = ba[:, 2 * N_HEADS:4 * N_HEADS].T.reshape(2 * N_HEADS, batch, nc, 1, CHUNK)
    cq = col_q // hb_count
    tok = lambda k: pl.BlockSpec((None, seq, bw), lambda b, h: (b, 0, cq + k * n_hblk + h))
    cw = lambda k: pl.BlockSpec((3, bw), lambda b, h: (0, k * n_hblk + h))
    in_specs = [tok(0), tok(1), tok(2), tok(3),
                pl.BlockSpec((None, seq, LANES), lambda b, h: (b, 0, 0)),
                pl.BlockSpec((2 * N_HEADS, None, nc, 1, CHUNK), lambda b, h: (0, b, 0, 0, 0)),
                cw(0), cw(1), cw(2),
                pl.BlockSpec((1, HEAD_DIM), lambda b, h: (0, 0))]
    args = [proj3, proj3, proj3, proj3, ba3, bat, conv_qkv_w, conv_qkv_w, conv_qkv_w,
            head_norm_g.reshape(1, HEAD_DIM)]
    state_spec = pl.BlockSpec((None, 2, hb_count, HEAD_DIM, HEAD_DIM), lambda b, h: (b, 0, h, 0, 0))
    if s0 is not None:
        in_specs.append(state_spec)
        args.append(s0)
    head_scr = pltpu.VMEM((hb_count, seq, HEAD_DIM), F32)
    dir_scr = pltpu.VMEM((2, hb_count, seq, LANES), F32)
    y, s_out = pl.pallas_call(
        functools.partial(_delta_kernel, seq=seq, hb_count=hb_count, zero_init=s0 is None),
        grid=(batch, n_hblk),
        in_specs=in_specs,
        out_specs=[pl.BlockSpec((None, seq, bw), lambda b, h: (b, 0, h)), state_spec],
        out_shape=[jax.ShapeDtypeStruct((batch, seq, kd), BF16),
                   jax.ShapeDtypeStruct((batch, 2, N_HEADS, HEAD_DIM, HEAD_DIM), F32)],
        scratch_shapes=[head_scr, head_scr, head_scr, dir_scr, dir_scr, head_scr,
                        pltpu.VMEM((2, hb_count, HEAD_DIM, HEAD_DIM), F32)],
        compiler_params=_params(("parallel", "parallel")),
        name="delta",
    )(*args)
    return y.reshape(batch * seq, kd), s_out


def _out_proj_kernel(x_ref, ya_ref, yb_ref, ga_ref, gb_ref, mod_ref, wpa_ref, wpb_ref, wo_ref,
                     fg_ref, o_ref, *, d, final):
    gate = mod_ref[:, 2 * d:3 * d]
    out_a = jnp.dot(ya_ref[...], wpa_ref[...], preferred_element_type=F32)
    out_b = jnp.dot(yb_ref[...], wpb_ref[...], preferred_element_type=F32)
    merged = (_sigmoid(ga_ref[...].astype(F32)) * out_a
              + _sigmoid(gb_ref[...].astype(F32)) * out_b)
    x = x_ref[...] + gate * jnp.dot(merged.astype(BF16), wo_ref[...], preferred_element_type=F32)
    if final:
        x = x * lax.rsqrt(jnp.mean(x * x, axis=-1, keepdims=True) + EPS) * fg_ref[...]
    o_ref[...] = x


def _out_proj_call(x, ya, yb, proj, mod, w_pa, w_pb, w_o, final_g, tokens_per_mod, final):
    n, d = x.shape
    cc = ya.shape[1]
    tm = 256
    kern = functools.partial(_out_proj_kernel, d=d, final=final)
    resident = lambda shape: pl.BlockSpec(shape, lambda i: (0, 0), pipeline_mode=pl.Buffered(1))
    return pl.pallas_call(
        kern,
        grid=(n // tm,),
        in_specs=[pl.BlockSpec((tm, d), lambda i: (i, 0)),
                  pl.BlockSpec((tm, cc), lambda i: (i, 0)),
                  pl.BlockSpec((tm, cc), lambda i: (i, 0)),
                  pl.BlockSpec((tm, d), lambda i: (i, 0)),
                  pl.BlockSpec((tm, d), lambda i: (i, 1)),
                  pl.BlockSpec((None, 1, 3 * d), lambda i: (_mod_row(i, tm, tokens_per_mod), 0, 0)),
                  resident(w_pa.shape), resident(w_pb.shape), resident(w_o.shape),
                  pl.BlockSpec((1, d), lambda i: (0, 0))],
        out_specs=pl.BlockSpec((tm, d), lambda i: (i, 0)),
        out_shape=jax.ShapeDtypeStruct((n, d), F32),
        compiler_params=_params(("parallel",)),
        name="out_proj",
    )(x, ya, yb, proj, proj, mod, w_pa, w_pb, w_o, final_g)


def kernel(x_prompt, x_sample, state_delta, c, c_ctx, w_mod, b_mod, norm_g, w_in, conv_a_w,
           conv_a_b, ln_a_g, ln_a_b, w_pa, conv_qkv_w, a_log, dt_bias, head_norm_g, w_pb, w_o,
           final_norm_g):
    batch, seq, d = x_prompt.shape
    dec_batch, dec_seq, _ = x_sample.shape
    depth = w_mod.shape[0]
    cc = conv_a_w.shape[2]
    kd = N_HEADS * HEAD_DIM
    o_ba = 3 * cc + 4 * kd
    o_gate = o_ba + 4 * N_HEADS

    cv = jnp.concatenate([c_ctx[None, :], c, jnp.zeros((MOD_ROWS - 1 - dec_batch, d), F32)], axis=0)
    mod = _mod_call(cv, w_mod, b_mod).reshape(depth, MOD_ROWS, 1, 3 * d)

    w_main = jnp.concatenate([w_in[:, :, o_gate:], w_in[:, :, :o_ba]], axis=2).astype(BF16)
    w_ba = jnp.pad(w_in[:, :, o_ba:o_gate], ((0, 0), (0, 0), (0, LANES - 4 * N_HEADS))).astype(BF16)
    gate_pad = ((0, 0), (2 * N_HEADS, LANES - 4 * N_HEADS))
    alog_rows = jnp.pad(a_log.reshape(depth, 2 * N_HEADS), gate_pad).reshape(depth, 1, LANES)
    dtb_rows = jnp.pad(dt_bias.reshape(depth, 2 * N_HEADS), gate_pad).reshape(depth, 1, LANES)
    w_pa16, w_pb16, w_o16 = w_pa.astype(BF16), w_pb.astype(BF16), w_o.astype(BF16)
    col_conv = (2 * d) // cc
    col_q = (2 * d + 3 * cc) // HEAD_DIM
    final_g = final_norm_g.reshape(1, d)

    def layer(x, l, latent):
        b, s = (dec_batch, dec_seq) if latent else (batch, seq)
        tokens_per_mod = dec_seq if latent else None
        proj, ba = _in_proj_call(x, mod[l], norm_g[l].reshape(1, d), w_main[l], w_ba[l],
                                 alog_rows[l], dtb_rows[l], tokens_per_mod)
        ya = _conv_call(proj, conv_a_w[l], conv_a_b[l], ln_a_g[l], ln_a_b[l], b, s, col_conv,
                        axial=latent)
        s0 = state_delta[:, l] if latent else None
        yb, s_out = _delta_call(proj, ba, conv_qkv_w[l], head_norm_g[l], s0, b, s, col_q)
        x = _out_proj_call(x, ya, yb, proj, mod[l], w_pa16[l], w_pb16[l], w_o16[l], final_g,
                           tokens_per_mod, final=(l == depth - 1))
        return x, s_out

    h = x_prompt.reshape(batch * seq, d)
    hs = x_sample.reshape(dec_batch * dec_seq, d)
    ctx_states = []
    for l in range(depth):
        h, st = layer(h, l, latent=False)
        ctx_states.append(st)
        hs, _ = layer(hs, l, latent=True)
    y_prompt = h.reshape(batch, seq, d)
    y_sample = hs.reshape(dec_batch, dec_seq, d)
    state_delta_new = jnp.stack(ctx_states, axis=1)
    return (y_prompt, y_sample, state_delta_new)
```

```python
import functools

import jax
import jax.numpy as jnp
from jax import lax
from jax.experimental import pallas as pl
from jax.experimental.pallas import tpu as pltpu

F32 = jnp.float32
BF16 = jnp.bfloat16

EPS = 1e-6
GRID_W = 64
CHUNK = 64
N_HEADS = 8
HEAD_DIM = 128
LANES = 128
MOD_ROWS = 8
PHASE1_PROBLEMS = 8
V7X_VMEM_LIMIT_BYTES = 56 * 1024 * 1024


def _sigmoid(x):
    return jax.nn.sigmoid(x)


def _silu(x):
    return x * jax.nn.sigmoid(x)


def _params(semantics):
    return pltpu.CompilerParams(dimension_semantics=semantics,
                                vmem_limit_bytes=V7X_VMEM_LIMIT_BYTES)


def _dot(a, b):
    return jnp.dot(a.astype(BF16), b.astype(BF16), preferred_element_type=F32)


def _dot_nt(a, b):
    return lax.dot_general(a.astype(BF16), b.astype(BF16), (((1,), (1,)), ((), ())),
                           preferred_element_type=F32)


def _mod_kernel(cv_ref, w_ref, b_ref, o_ref):
    a = _silu(cv_ref[...])
    o_ref[...] = _dot(a, w_ref[...]) + b_ref[...]


def _mod_call(cv, w_mod, b_mod):
    depth, d, d3 = w_mod.shape
    tn = 1024
    return pl.pallas_call(
        _mod_kernel,
        grid=(depth, d3 // tn),
        in_specs=[pl.BlockSpec((MOD_ROWS, d), lambda l, j: (0, 0)),
                  pl.BlockSpec((None, d, tn), lambda l, j: (l, 0, j)),
                  pl.BlockSpec((None, 1, tn), lambda l, j: (l, 0, j))],
        out_specs=pl.BlockSpec((None, MOD_ROWS, tn), lambda l, j: (l, 0, j)),
        out_shape=jax.ShapeDtypeStruct((depth, MOD_ROWS, d3), F32),
        compiler_params=_params(("parallel", "parallel")),
        name="mod",
    )(cv, w_mod, b_mod.reshape(depth, 1, d3))


def _mod_row(i, tm, tokens_per_mod):
    if tokens_per_mod is None:
        return 0
    return 1 + (i * tm) // tokens_per_mod


def _in_proj_kernel(x_ref, mod_ref, ng_ref, w_ref, wba_ref, alog_ref, dtb_ref,
                    proj_ref, ba_ref, hn_ref, *, tm, d):
    j = pl.program_id(1)

    @pl.when(j == 0)
    def _():
        shift = mod_ref[:, 0:d]
        scale = mod_ref[:, d:2 * d]
        gain = ng_ref[...] * (1.0 + scale)
        rb = 64

        def norm_rows(t, carry):
            rows = pl.ds(pl.multiple_of(t * rb, rb), rb)
            x = x_ref[rows, :]
            inv = lax.rsqrt(jnp.mean(x * x, axis=-1, keepdims=True) + EPS)
            hn_ref[rows, :] = (x * inv * gain + shift).astype(BF16)
            return carry

        lax.fori_loop(0, tm // rb, norm_rows, 0)

        pba = jnp.dot(hn_ref[...], wba_ref[...], preferred_element_type=F32)
        beta = _sigmoid(pba)
        z = pba + dtb_ref[...]
        softplus = jnp.maximum(z, 0.0) + jnp.log(1.0 + jnp.exp(-jnp.abs(z)))
        g = -jnp.exp(alog_ref[...]) * softplus
        pos = lax.broadcasted_iota(jnp.int32, (tm, LANES), 0) & (CHUNK - 1)
        pre = g
        suf = g
        s = 1
        while s < CHUNK:
            pre = pre + jnp.where(pos >= s, pltpu.roll(pre, s, 0), 0.0)
            suf = suf + jnp.where(pos < CHUNK - s, pltpu.roll(suf, tm - s, 0), 0.0)
            s *= 2
        lane = lax.broadcasted_iota(jnp.int32, (tm, LANES), 1)
        ba_ref[...] = jnp.where(lane < 2 * N_HEADS, beta,
                                jnp.where(lane < 3 * N_HEADS, pre, suf))

    proj_ref[...] = jnp.dot(hn_ref[...], w_ref[...],
                            preferred_element_type=F32).astype(proj_ref.dtype)


def _in_proj_call(x, mod, norm_g, w_main, w_ba, alog_row, dtb_row, tokens_per_mod):
    n, d = x.shape
    width = w_main.shape[1]
    tm, tn = 512, 512
    kern = functools.partial(_in_proj_kernel, tm=tm, d=d)
    return pl.pallas_call(
        kern,
        grid=(n // tm, width // tn),
        in_specs=[pl.BlockSpec((tm, d), lambda i, j: (i, 0)),
                  pl.BlockSpec((None, 1, 3 * d), lambda i, j: (_mod_row(i, tm, tokens_per_mod), 0, 0)),
                  pl.BlockSpec((1, d), lambda i, j: (0, 0)),
                  pl.BlockSpec((d, tn), lambda i, j: (0, j)),
                  pl.BlockSpec((d, LANES), lambda i, j: (0, 0)),
                  pl.BlockSpec((1, LANES), lambda i, j: (0, 0)),
                  pl.BlockSpec((1, LANES), lambda i, j: (0, 0))],
        out_specs=[pl.BlockSpec((tm, tn), lambda i, j: (i, j)),
                   pl.BlockSpec((tm, LANES), lambda i, j: (i, 0))],
        out_shape=[jax.ShapeDtypeStruct((n, width), BF16),
                   jax.ShapeDtypeStruct((n, LANES), F32)],
        scratch_shapes=[pltpu.VMEM((tm, d), BF16)],
        compiler_params=_params(("parallel", "arbitrary")),
        name="in_proj",
    )(x, mod, norm_g, w_main, w_ba, alog_row, dtb_row)


def _ln_swish_gate(conv_ref, az_ref, g_ref, b_ref, o_ref, n_rows):
    rb = 32
    gain = g_ref[...]
    bias = b_ref[...]

    def rows_body(t, carry):
        rows = pl.ds(pl.multiple_of(t * rb, rb), rb)
        z = conv_ref[rows, :]
        mu = jnp.mean(z, axis=-1, keepdims=True)
        zc = z - mu
        var = jnp.mean(zc * zc, axis=-1, keepdims=True)
        y = _silu(zc * lax.rsqrt(var + EPS) * gain + bias)
        o_ref[rows, :] = (y * _silu(az_ref[rows, :].astype(F32))).astype(o_ref.dtype)
        return carry

    lax.fori_loop(0, n_rows // rb, rows_body, 0)


def _conv_seq_kernel(av_ref, ag_ref, az_ref, w_ref, cb_ref, g_ref, b_ref, o_ref,
                     pad_ref, conv_ref, *, seq, taps):
    half = taps // 2
    lead = 16
    ch = av_ref.shape[1]
    pad_ref[0:lead, :] = jnp.zeros((lead, ch), F32)
    pad_ref[lead + seq:lead + seq + lead, :] = jnp.zeros((lead, ch), F32)
    pad_ref[lead:lead + seq, :] = av_ref[...].astype(F32) * _sigmoid(ag_ref[...].astype(F32))
    rb = 128

    def lane_block(c, carry):
        lanes = pl.ds(pl.multiple_of(c * LANES, LANES), LANES)
        for r in range(seq // rb):
            acc = jnp.zeros((rb, LANES), F32)
            for j in range(taps):
                acc = acc + w_ref[j:j + 1, lanes] * pad_ref[pl.ds(r * rb + lead - half + j, rb), lanes]
            conv_ref[r * rb:(r + 1) * rb, lanes] = acc + cb_ref[:, lanes]
        return carry

    lax.fori_loop(0, ch // LANES, lane_block, 0)
    _ln_swish_gate(conv_ref, az_ref, g_ref, b_ref, o_ref, seq)


def _conv_axial_kernel(av_ref, ag_ref, az_ref, w_ref, cb_ref, g_ref, b_ref, o_ref,
                       pad_ref, ver_ref, conv_ref, *, grid_h, taps):
    half = taps // 2
    lead = 16
    ch = av_ref.shape[1]
    hc = ch // 2
    n_tok = grid_h * GRID_W
    for r in range(grid_h):
        rows = slice(r * GRID_W, (r + 1) * GRID_W)
        ua = av_ref[rows, :].astype(F32) * _sigmoid(ag_ref[rows, :].astype(F32))
        pad_ref[r, 0:lead, :] = jnp.zeros((lead, hc), F32)
        pad_ref[r, lead + GRID_W:lead + GRID_W + lead, :] = jnp.zeros((lead, hc), F32)
        pad_ref[r, lead:lead + GRID_W, :] = ua[:, 0:hc]
        ver_ref[rows, :] = ua[:, hc:ch]

    def lane_block(c, carry):
        lanes = pl.ds(pl.multiple_of(c * LANES, LANES), LANES)
        lanes_v = pl.ds(pl.multiple_of(hc + c * LANES, LANES), LANES)

        def grid_row(r, carry2):
            out_rows = pl.ds(pl.multiple_of(r * GRID_W, GRID_W), GRID_W)
            acc = jnp.zeros((GRID_W, LANES), F32)
            for j in range(taps):
                acc = acc + w_ref[j:j + 1, lanes] * pad_ref[r, pl.ds(lead - half + j, GRID_W), lanes]
            conv_ref[out_rows, lanes] = acc + cb_ref[:, lanes]
            return carry2

        lax.fori_loop(0, grid_h, grid_row, 0)
        for r in range(grid_h):
            acc = jnp.zeros((GRID_W, LANES), F32)
            for src in range(grid_h):
                tap = src + half - r
                acc = acc + (w_ref[tap:tap + 1, lanes_v]
                             * ver_ref[src * GRID_W:(src + 1) * GRID_W, lanes])
            conv_ref[r * GRID_W:(r + 1) * GRID_W, lanes_v] = acc + cb_ref[:, lanes_v]
        return carry

    lax.fori_loop(0, hc // LANES, lane_block, 0)
    _ln_swish_gate(conv_ref, az_ref, g_ref, b_ref, o_ref, n_tok)


def _conv_call(proj, conv_w, conv_b, ln_g, ln_b, batch, seq, col0, axial):
    taps, ch = conv_w.shape
    proj3 = proj.reshape(batch, seq, proj.shape[1])
    row = lambda v: v.reshape(1, ch)
    if axial:
        grid_h = seq // GRID_W
        assert taps // 2 >= grid_h - 1, "every grid row must lie inside the conv window"
        kern = functools.partial(_conv_axial_kernel, grid_h=grid_h, taps=taps)
        scratch = [pltpu.VMEM((grid_h, GRID_W + 32, ch // 2), F32),
                   pltpu.VMEM((seq, ch // 2), F32),
                   pltpu.VMEM((seq, ch), F32)]
    else:
        kern = functools.partial(_conv_seq_kernel, seq=seq, taps=taps)
        scratch = [pltpu.VMEM((seq + 32, ch), F32), pltpu.VMEM((seq, ch), F32)]
    tok = lambda k: pl.BlockSpec((None, seq, ch), lambda b: (b, 0, col0 + k))
    const = lambda shape: pl.BlockSpec(shape, lambda b: (0, 0))
    out = pl.pallas_call(
        kern,
        grid=(batch,),
        in_specs=[tok(0), tok(1), tok(2), const((taps, ch)), const((1, ch)), const((1, ch)),
                  const((1, ch))],
        out_specs=pl.BlockSpec((None, seq, ch), lambda b: (b, 0, 0)),
        out_shape=jax.ShapeDtypeStruct((batch, seq, ch), BF16),
        scratch_shapes=scratch,
        compiler_params=_params(("parallel",)),
        name="conv_axial" if axial else "conv_seq",
    )(proj3, proj3, proj3, conv_w, row(conv_b), row(ln_g), row(ln_b))
    return out.reshape(batch * seq, ch)


def _split_bf16(x):
    hi = x.astype(BF16)
    return hi, (x - hi.astype(F32)).astype(BF16)


def _dots_split(pairs):
    parts = [(_split_bf16(a), _split_bf16(b)) for a, b in pairs]
    by_hi = [jnp.dot(jnp.concatenate([a_hi, a_lo], axis=0), b_hi, preferred_element_type=F32)
             for (a_hi, a_lo), (b_hi, _) in parts]
    by_lo = [jnp.dot(a_hi, b_lo, preferred_element_type=F32) for (a_hi, _), (_, b_lo) in parts]
    out = []
    for (a, _), hi, lo in zip(pairs, by_hi, by_lo):
        m = a.shape[0]
        out.append(hi[0:m] + hi[m:2 * m] + lo)
    return out


def _block_diag(x, left):
    return jnp.concatenate([jnp.where(left, x, 0.0), jnp.where(left, 0.0, x)], axis=0)


def _unit_triangular_inverse_pairs(n2s, eye2, left):
    ps = [eye2 - n2 for n2 in n2s]
    ss = _dots_split([(n2, _block_diag(n2, left)) for n2 in n2s])
    power = 2
    while 2 * power < CHUNK:
        prods = _dots_split([(jnp.concatenate([p, s], axis=0), _block_diag(s, left))
                             for p, s in zip(ps, ss)])
        ps = [p + pr[0:CHUNK] for p, pr in zip(ps, prods)]
        ss = [pr[CHUNK:2 * CHUNK] for pr in prods]
        power *= 2
    last = _dots_split([(p, _block_diag(s, left)) for p, s in zip(ps, ss)])
    return [p + pr for p, pr in zip(ps, last)]


def _delta_kernel(*refs, seq, hb_count, chunk_unroll, zero_init):
    if zero_init:
        (q_ref, k_ref, v_ref, z_ref, ba_ref, bat_ref, wq_ref, wk_ref, wv_ref, hng_ref,
         y_ref, sout_ref, qs, ks, vs, gates, u_scr, wq_scr, lx_scr, egl_scr, o_scr, sscr) = refs
        s0_ref = None
    else:
        (q_ref, k_ref, v_ref, z_ref, ba_ref, bat_ref, wq_ref, wk_ref, wv_ref, hng_ref, s0_ref,
         y_ref, sout_ref, qs, ks, vs, gates, u_scr, wq_scr, lx_scr, egl_scr, o_scr, sscr) = refs
    nc = seq // CHUNK
    wq_rows = 2 * CHUNK
    lx_rows = CHUNK + HEAD_DIM
    h0 = pl.program_id(1) * hb_count

    row = lax.broadcasted_iota(jnp.int32, (seq, HEAD_DIM), 0)
    lane = lax.broadcasted_iota(jnp.int32, (seq, LANES), 1)

    def conv_swish(x_ref, w_ref, cols):
        x = x_ref[:, cols].astype(F32)
        prev = jnp.where(row >= 1, pltpu.roll(x, 1, 0), 0.0)
        nxt = jnp.where(row <= seq - 2, pltpu.roll(x, seq - 1, 0), 0.0)
        return _silu(w_ref[0:1, cols] * prev + w_ref[1:2, cols] * x + w_ref[2:3, cols] * nxt)

    def l2norm(x):
        return x * lax.rsqrt(jnp.sum(x * x, axis=-1, keepdims=True) + EPS)

    ba = ba_ref[...]
    for hb in range(hb_count):
        cols = slice(hb * HEAD_DIM, (hb + 1) * HEAD_DIM)
        qs[hb] = l2norm(conv_swish(q_ref, wq_ref, cols)) * (HEAD_DIM ** -0.5)
        ks[hb] = l2norm(conv_swish(k_ref, wk_ref, cols))
        vs[hb] = conv_swish(v_ref, wv_ref, cols)
        for t in range(4):
            col = jnp.sum(jnp.where(lane == t * N_HEADS + h0 + hb, ba, 0.0), axis=-1, keepdims=True)
            gates[t, hb] = jnp.broadcast_to(col, (seq, LANES))
        for d in range(2):
            if zero_init:
                sscr[d, hb] = jnp.zeros((HEAD_DIM, HEAD_DIM), F32)
            else:
                sscr[d, hb] = s0_ref[d, hb]

    ii = lax.broadcasted_iota(jnp.int32, (CHUNK, LANES), 0)
    l64 = lax.broadcasted_iota(jnp.int32, (CHUNK, LANES), 1)
    left = l64 < CHUNK
    right = l64 >= CHUNK
    jj = l64 & (CHUNK - 1)
    below = jnp.where(left, ii - jj, jj - ii)
    incl = below >= 0
    strict = below > 0
    eye2 = jnp.where(ii == jj, 1.0, 0.0).astype(F32)
    lane_t = lax.broadcasted_iota(jnp.int32, (HEAD_DIM, LANES), 1)
    left_t = lane_t < CHUNK
    right_t = lane_t >= CHUNK
    zeros = jnp.zeros((CHUNK, HEAD_DIM), F32)

    def chunk_products(problems):
        kqs = [_dot_nt(jnp.concatenate([k, q], axis=0), jnp.concatenate([k, k], axis=0))
               for q, k, *_ in problems]
        decays = []
        n2s = []
        for (q, k, v, bf, bb, gf, gb, grow2), kq in zip(problems, kqs):
            diff = jnp.where(left, gf, gb) - grow2
            decay = jnp.where(incl, jnp.exp(jnp.where(incl, diff, 0.0)), 0.0)
            decays.append(decay)
            n2s.append(jnp.where(strict, jnp.where(left, bf, bb) * kq[0:CHUNK] * decay, 0.0))
        tinvs = _unit_triangular_inverse_pairs(n2s, eye2, left)
        rhss = []
        for q, k, v, bf, bb, gf, gb, grow2 in problems:
            rhss.append(jnp.concatenate(
                [jnp.concatenate([v * bf, k * bf * jnp.exp(gf), zeros, zeros], axis=1),
                 jnp.concatenate([zeros, zeros, v * bb, k * bb * jnp.exp(gb)], axis=1)], axis=0))
        uws = [_dot(tinv, rhs) for tinv, rhs in zip(tinvs, rhss)]
        results = []
        for (q, k, v, bf, bb, gf, gb, grow2), kq, decay, uw in zip(problems, kqs, decays, uws):
            glf = gf[CHUNK - 1:CHUNK, :]
            glb = gb[0:1, :]
            kt2 = jnp.concatenate([k * jnp.exp(glf - gf), k * jnp.exp(glb - gb)], axis=0).T
            qkm = kq[CHUNK:2 * CHUNK] * decay
            out = []
            for d, (g, gl) in enumerate(((gf, glf), (gb, glb))):
                keep = left if d == 0 else right
                keep_t = left_t if d == 0 else right_t
                u = uw[:, (2 * d) * HEAD_DIM:(2 * d + 1) * HEAD_DIM]
                w = uw[:, (2 * d + 1) * HEAD_DIM:(2 * d + 2) * HEAD_DIM]
                wq = jnp.concatenate([w, q * jnp.exp(g)], axis=0).astype(BF16)
                lx = jnp.concatenate([jnp.where(keep, qkm, 0.0), jnp.where(keep_t, kt2, 0.0)],
                                     axis=0).astype(BF16)
                out.append((u, wq, lx, jnp.broadcast_to(jnp.exp(gl), (8, LANES))))
            results.append(out)
        return results

    def phase1(i, carry):
        where = []
        problems = []
        for j in range(chunk_unroll):
            c = i * chunk_unroll + j
            rows = pl.ds(pl.multiple_of(c * CHUNK, CHUNK), CHUNK)
            for hb in range(hb_count):
                where.append((c, rows, hb))
                problems.append((qs[hb, rows, :], ks[hb, rows, :], vs[hb, rows, :],
                                 gates[0, hb, rows, :], gates[1, hb, rows, :],
                                 gates[2, hb, rows, :], gates[3, hb, rows, :],
                                 bat_ref[h0 + hb, c]))
        for (c, rows, hb), res in zip(where, chunk_products(problems)):
            for d, (u, wq, lx, egl) in enumerate(res):
                u_scr[d, hb, rows, :] = u
                wq_scr[d, hb, pl.ds(pl.multiple_of(c * wq_rows, wq_rows), wq_rows), :] = wq
                lx_scr[d, hb, pl.ds(pl.multiple_of(c * lx_rows, CHUNK), lx_rows), :] = lx
                egl_scr[d, hb, c] = egl
        return carry

    lax.fori_loop(0, nc // chunk_unroll, phase1, 0)

    def phase2(c, carry):
        work = []
        for hb in range(hb_count):
            for d in range(2):
                cd = c if d == 0 else nc - 1 - c
                rows = pl.ds(pl.multiple_of(cd * CHUNK, CHUNK), CHUNK)
                work.append((hb, d, rows, sscr[d, hb], u_scr[d, hb, rows, :],
                             wq_scr[d, hb, pl.ds(pl.multiple_of(cd * wq_rows, wq_rows), wq_rows), :],
                             lx_scr[d, hb, pl.ds(pl.multiple_of(cd * lx_rows, CHUNK), lx_rows), :],
                             egl_scr[d, hb, cd]))
        wss = [jnp.dot(wq, state.astype(BF16), preferred_element_type=F32)
               for _, _, _, state, _, wq, _, _ in work]
        v_news = [(u - ws[0:CHUNK]).astype(BF16)
                  for (_, _, _, _, u, _, _, _), ws in zip(work, wss)]
        oms = [jnp.dot(lx, jnp.concatenate([v_new, v_new], axis=0), preferred_element_type=F32)
               for (_, _, _, _, _, _, lx, _), v_new in zip(work, v_news)]
        for (hb, d, rows, state, _, _, _, egl), ws, om in zip(work, wss, oms):
            o_scr[d, hb, rows, :] = ws[CHUNK:2 * CHUNK] + om[0:CHUNK]
            sscr[d, hb] = state * egl[0:1, :] + om[CHUNK:lx_rows]
        return carry

    lax.fori_loop(0, nc, phase2, 0)

    for hb in range(hb_count):
        cols = slice(hb * HEAD_DIM, (hb + 1) * HEAD_DIM)
        o = o_scr[0, hb] + o_scr[1, hb]
        o = o * lax.rsqrt(jnp.mean(o * o, axis=-1, keepdims=True) + EPS) * hng_ref[...]
        y_ref[:, cols] = (o * _silu(z_ref[:, cols].astype(F32))).astype(y_ref.dtype)
        for d in range(2):
            sout_ref[d, hb] = sscr[d, hb]


def _delta_call(proj, ba, conv_qkv_w, head_norm_g, s0, batch, seq, col_q):
    nc = seq // CHUNK
    hb_count = 4 if nc <= 4 else 2
    chunk_unroll = PHASE1_PROBLEMS // hb_count
    n_hblk = N_HEADS // hb_count
    bw = hb_count * HEAD_DIM
    kd = N_HEADS * HEAD_DIM
    proj3 = proj.reshape(batch, seq, proj.shape[1])
    ba3 = ba.reshape(batch, seq, LANES)
    g_rows = lambda lo: ba[:, lo:lo + N_HEADS].T.reshape(N_HEADS, batch, nc, 1, CHUNK)
    bat = jnp.concatenate([g_rows(2 * N_HEADS), g_rows(3 * N_HEADS)], axis=-1)
    cq = col_q // hb_count
    tok = lambda k: pl.BlockSpec((None, seq, bw), lambda b, h: (b, 0, cq + k * n_hblk + h))
    cw = lambda k: pl.BlockSpec((3, bw), lambda b, h: (0, k * n_hblk + h))
    in_specs = [tok(0), tok(1), tok(2), tok(3),
                pl.BlockSpec((None, seq, LANES), lambda b, h: (b, 0, 0)),
                pl.BlockSpec((N_HEADS, None, nc, 1, LANES), lambda b, h: (0, b, 0, 0, 0)),
                cw(0), cw(1), cw(2),
                pl.BlockSpec((1, HEAD_DIM), lambda b, h: (0, 0))]
    args = [proj3, proj3, proj3, proj3, ba3, bat, conv_qkv_w, conv_qkv_w, conv_qkv_w,
            head_norm_g.reshape(1, HEAD_DIM)]
    state_spec = pl.BlockSpec((None, 2, hb_count, HEAD_DIM, HEAD_DIM), lambda b, h: (b, 0, h, 0, 0))
    if s0 is not None:
        in_specs.append(state_spec)
        args.append(s0)
    head_scr = pltpu.VMEM((hb_count, seq, HEAD_DIM), F32)
    dir_scr = pltpu.VMEM((2, hb_count, seq, HEAD_DIM), F32)
    y, s_out = pl.pallas_call(
        functools.partial(_delta_kernel, seq=seq, hb_count=hb_count, chunk_unroll=chunk_unroll,
                          zero_init=s0 is None),
        grid=(batch, n_hblk),
        in_specs=in_specs,
        out_specs=[pl.BlockSpec((None, seq, bw), lambda b, h: (b, 0, h)), state_spec],
        out_shape=[jax.ShapeDtypeStruct((batch, seq, kd), BF16),
                   jax.ShapeDtypeStruct((batch, 2, N_HEADS, HEAD_DIM, HEAD_DIM), F32)],
        scratch_shapes=[head_scr, head_scr, head_scr,
                        pltpu.VMEM((4, hb_count, seq, LANES), F32),
                        dir_scr,
                        pltpu.VMEM((2, hb_count, nc * 2 * CHUNK, HEAD_DIM), BF16),
                        pltpu.VMEM((2, hb_count, nc * (CHUNK + HEAD_DIM), LANES), BF16),
                        pltpu.VMEM((2, hb_count, nc, 8, LANES), F32),
                        dir_scr,
                        pltpu.VMEM((2, hb_count, HEAD_DIM, HEAD_DIM), F32)],
        compiler_params=_params(("parallel", "parallel")),
        name="delta",
    )(*args)
    return y.reshape(batch * seq, kd), s_out


def _out_proj_kernel(x_ref, ya_ref, yb_ref, ga_ref, gb_ref, mod_ref, wpa_ref, wpb_ref, wo_ref,
                     fg_ref, o_ref, *, d, final):
    gate = mod_ref[:, 2 * d:3 * d]
    out_a = jnp.dot(ya_ref[...], wpa_ref[...], preferred_element_type=F32)
    out_b = jnp.dot(yb_ref[...], wpb_ref[...], preferred_element_type=F32)
    merged = (_sigmoid(ga_ref[...].astype(F32)) * out_a
              + _sigmoid(gb_ref[...].astype(F32)) * out_b)
    x = x_ref[...] + gate * jnp.dot(merged.astype(BF16), wo_ref[...], preferred_element_type=F32)
    if final:
        x = x * lax.rsqrt(jnp.mean(x * x, axis=-1, keepdims=True) + EPS) * fg_ref[...]
    o_ref[...] = x


def _out_proj_call(x, ya, yb, proj, mod, w_pa, w_pb, w_o, final_g, tokens_per_mod, final):
    n, d = x.shape
    cc = ya.shape[1]
    tm = 256
    kern = functools.partial(_out_proj_kernel, d=d, final=final)
    resident = lambda shape: pl.BlockSpec(shape, lambda i: (0, 0), pipeline_mode=pl.Buffered(1))
    return pl.pallas_call(
        kern,
        grid=(n // tm,),
        in_specs=[pl.BlockSpec((tm, d), lambda i: (i, 0)),
                  pl.BlockSpec((tm, cc), lambda i: (i, 0)),
                  pl.BlockSpec((tm, cc), lambda i: (i, 0)),
                  pl.BlockSpec((tm, d), lambda i: (i, 0)),
                  pl.BlockSpec((tm, d), lambda i: (i, 1)),
                  pl.BlockSpec((None, 1, 3 * d), lambda i: (_mod_row(i, tm, tokens_per_mod), 0, 0)),
                  resident(w_pa.shape), resident(w_pb.shape), resident(w_o.shape),
                  pl.BlockSpec((1, d), lambda i: (0, 0))],
        out_specs=pl.BlockSpec((tm, d), lambda i: (i, 0)),
        out_shape=jax.ShapeDtypeStruct((n, d), F32),
        compiler_params=_params(("parallel",)),
        name="out_proj",
    )(x, ya, yb, proj, proj, mod, w_pa, w_pb, w_o, final_g)


def kernel(x_prompt, x_sample, state_delta, c, c_ctx, w_mod, b_mod, norm_g, w_in, conv_a_w,
           conv_a_b, ln_a_g, ln_a_b, w_pa, conv_qkv_w, a_log, dt_bias, head_norm_g, w_pb, w_o,
           final_norm_g):
    batch, seq, d = x_prompt.shape
    dec_batch, dec_seq, _ = x_sample.shape
    depth = w_mod.shape[0]
    cc = conv_a_w.shape[2]
    kd = N_HEADS * HEAD_DIM
    o_ba = 3 * cc + 4 * kd
    o_gate = o_ba + 4 * N_HEADS

    cv = jnp.concatenate([c_ctx[None, :], c, jnp.zeros((MOD_ROWS - 1 - dec_batch, d), F32)], axis=0)
    mod = _mod_call(cv, w_mod, b_mod).reshape(depth, MOD_ROWS, 1, 3 * d)

    w_main = jnp.concatenate([w_in[:, :, o_gate:], w_in[:, :, :o_ba]], axis=2).astype(BF16)
    w_ba = jnp.pad(w_in[:, :, o_ba:o_gate], ((0, 0), (0, 0), (0, LANES - 4 * N_HEADS))).astype(BF16)
    gate_pad = ((0, 0), (2 * N_HEADS, LANES - 4 * N_HEADS))
    alog_rows = jnp.pad(a_log.reshape(depth, 2 * N_HEADS), gate_pad).reshape(depth, 1, LANES)
    dtb_rows = jnp.pad(dt_bias.reshape(depth, 2 * N_HEADS), gate_pad).reshape(depth, 1, LANES)
    w_pa16, w_pb16, w_o16 = w_pa.astype(BF16), w_pb.astype(BF16), w_o.astype(BF16)
    col_conv = (2 * d) // cc
    col_q = (2 * d + 3 * cc) // HEAD_DIM
    final_g = final_norm_g.reshape(1, d)

    def layer(x, l, latent):
        b, s = (dec_batch, dec_seq) if latent else (batch, seq)
        tokens_per_mod = dec_seq if latent else None
        proj, ba = _in_proj_call(x, mod[l], norm_g[l].reshape(1, d), w_main[l], w_ba[l],
                                 alog_rows[l], dtb_rows[l], tokens_per_mod)
        ya = _conv_call(proj, conv_a_w[l], conv_a_b[l], ln_a_g[l], ln_a_b[l], b, s, col_conv,
                        axial=latent)
        s0 = state_delta[:, l] if latent else None
        yb, s_out = _delta_call(proj, ba, conv_qkv_w[l], head_norm_g[l], s0, b, s, col_q)
        x = _out_proj_call(x, ya, yb, proj, mod[l], w_pa16[l], w_pb16[l], w_o16[l], final_g,
                           tokens_per_mod, final=(l == depth - 1))
        return x, s_out

    h = x_prompt.reshape(batch * seq, d)
    hs = x_sample.reshape(dec_batch * dec_seq, d)
    ctx_states = []
    for l in range(depth):
        h, st = layer(h, l, latent=False)
        ctx_states.append(st)
        hs, _ = layer(hs, l, latent=True)
    y_prompt = h.reshape(batch, seq, d)
    y_sample = hs.reshape(dec_batch, dec_seq, d)
    state_delta_new = jnp.stack(ctx_states, axis=1)
    return (y_prompt, y_sample, state_delta_new)
```

```python
import functools

import jax
import jax.numpy as jnp
from jax import lax
from jax.experimental import pallas as pl
from jax.experimental.pallas import tpu as pltpu

F32 = jnp.float32
BF16 = jnp.bfloat16

EPS = 1e-6
GRID_W = 64
CHUNK = 64
N_HEADS = 8
HEAD_DIM = 128
LANES = 128
SUBLANES = 8
CONV_LEAD = 16
MOD_ROWS = 8
PHASE1_PROBLEMS = 8
V7X_VMEM_LIMIT_BYTES = 56 * 1024 * 1024


def _sigmoid(x):
    return jax.nn.sigmoid(x)


def _silu(x):
    return x * jax.nn.sigmoid(x)


def _params(semantics):
    return pltpu.CompilerParams(dimension_semantics=semantics,
                                vmem_limit_bytes=V7X_VMEM_LIMIT_BYTES)


def _dot(a, b):
    return jnp.dot(a.astype(BF16), b.astype(BF16), preferred_element_type=F32)


def _dot_nt(a, b):
    return lax.dot_general(a.astype(BF16), b.astype(BF16), (((1,), (1,)), ((), ())),
                           preferred_element_type=F32)


def _mod_kernel(cv_ref, w_ref, b_ref, o_ref):
    a = _silu(cv_ref[...])
    o_ref[...] = _dot(a, w_ref[...]) + b_ref[...]


def _mod_call(cv, w_mod, b_mod):
    depth, d, d3 = w_mod.shape
    tn = 1024
    return pl.pallas_call(
        _mod_kernel,
        grid=(depth, d3 // tn),
        in_specs=[pl.BlockSpec((MOD_ROWS, d), lambda l, j: (0, 0)),
                  pl.BlockSpec((None, d, tn), lambda l, j: (l, 0, j)),
                  pl.BlockSpec((None, 1, tn), lambda l, j: (l, 0, j))],
        out_specs=pl.BlockSpec((None, MOD_ROWS, tn), lambda l, j: (l, 0, j)),
        out_shape=jax.ShapeDtypeStruct((depth, MOD_ROWS, d3), F32),
        compiler_params=_params(("parallel", "parallel")),
        name="mod",
    )(cv, w_mod, b_mod.reshape(depth, 1, d3))


def _mod_row(i, tm, tokens_per_mod):
    if tokens_per_mod is None:
        return 0
    return 1 + (i * tm) // tokens_per_mod


def _in_proj_kernel(x_ref, mod_ref, ng_ref, w_ref, wg_ref, wba_ref, alog_ref, dtb_ref,
                    proj_ref, ba_ref, hn_ref, *, tm, d, n_main):
    j = pl.program_id(1)

    @pl.when(j == 0)
    def _():
        shift = mod_ref[:, 0:d]
        scale = mod_ref[:, d:2 * d]
        gain = ng_ref[...] * (1.0 + scale)
        rb = 64

        def norm_rows(t, carry):
            rows = pl.ds(pl.multiple_of(t * rb, rb), rb)
            x = x_ref[rows, :]
            inv = lax.rsqrt(jnp.mean(x * x, axis=-1, keepdims=True) + EPS)
            hn_ref[rows, :] = (x * inv * gain + shift).astype(BF16)
            return carry

        lax.fori_loop(0, tm // rb, norm_rows, 0)

        pba = jnp.dot(hn_ref[...], wba_ref[...], preferred_element_type=F32)
        beta = _sigmoid(pba)
        z = pba + dtb_ref[...]
        softplus = jnp.maximum(z, 0.0) + jnp.log(1.0 + jnp.exp(-jnp.abs(z)))
        g = -jnp.exp(alog_ref[...]) * softplus
        pos = lax.broadcasted_iota(jnp.int32, (tm, LANES), 0) & (CHUNK - 1)
        pre = g
        suf = g
        s = 1
        while s < CHUNK:
            pre = pre + jnp.where(pos >= s, pltpu.roll(pre, s, 0), 0.0)
            suf = suf + jnp.where(pos < CHUNK - s, pltpu.roll(suf, tm - s, 0), 0.0)
            s *= 2
        lane = lax.broadcasted_iota(jnp.int32, (tm, LANES), 1)
        ba_ref[...] = jnp.where(lane < 2 * N_HEADS, beta,
                                jnp.where(lane < 3 * N_HEADS, pre, suf))

    @pl.when(j < n_main)
    def _():
        proj_ref[...] = jnp.dot(hn_ref[...], w_ref[...].astype(BF16),
                                preferred_element_type=F32).astype(proj_ref.dtype)

    @pl.when(j >= n_main)
    def _():
        proj_ref[...] = jnp.dot(hn_ref[...], wg_ref[...],
                                preferred_element_type=F32).astype(proj_ref.dtype)


def _in_proj_call(x, mod, norm_g, w_in, w_gate, w_ba, alog_row, dtb_row, layer, main_width,
                  tokens_per_mod):
    n, d = x.shape
    gate_width = w_gate.shape[2]
    tm, tn = 1024, 512
    n_main = main_width // tn
    n_gate = gate_width // tn
    kern = functools.partial(_in_proj_kernel, tm=tm, d=d, n_main=n_main)
    return pl.pallas_call(
        kern,
        grid=(n // tm, n_main + n_gate),
        in_specs=[pl.BlockSpec((tm, d), lambda i, j: (i, 0)),
                  pl.BlockSpec((None, 1, 3 * d), lambda i, j: (_mod_row(i, tm, tokens_per_mod), 0, 0)),
                  pl.BlockSpec((1, d), lambda i, j: (0, 0)),
                  pl.BlockSpec((None, d, tn), lambda i, j: (layer, 0, jnp.minimum(j, n_main - 1))),
                  pl.BlockSpec((None, d, tn), lambda i, j: (layer, 0, jnp.maximum(j - n_main, 0))),
                  pl.BlockSpec((d, LANES), lambda i, j: (0, 0)),
                  pl.BlockSpec((1, LANES), lambda i, j: (0, 0)),
                  pl.BlockSpec((1, LANES), lambda i, j: (0, 0))],
        out_specs=[pl.BlockSpec((tm, tn),
                                lambda i, j: (i, jnp.where(j < n_main, j + n_gate, j - n_main))),
                   pl.BlockSpec((tm, LANES), lambda i, j: (i, 0))],
        out_shape=[jax.ShapeDtypeStruct((n, gate_width + main_width), BF16),
                   jax.ShapeDtypeStruct((n, LANES), F32)],
        scratch_shapes=[pltpu.VMEM((tm, d), BF16)],
        compiler_params=_params(("parallel", "arbitrary")),
        name="in_proj",
    )(x, mod, norm_g, w_in, w_gate, w_ba, alog_row, dtb_row)


def _ln_swish_gate(conv_ref, az_ref, g_ref, b_ref, o_ref, n_rows):
    rb = 16
    blocks = 4
    gain = g_ref[...]
    bias = b_ref[...]

    def rows_body(t, carry):
        rows = [pl.ds(pl.multiple_of((t * blocks + u) * rb, rb), rb) for u in range(blocks)]
        zs = [conv_ref[r, :] for r in rows]
        mus = [jnp.mean(z, axis=-1, keepdims=True) for z in zs]
        zcs = [z - mu for z, mu in zip(zs, mus)]
        variances = [jnp.mean(zc * zc, axis=-1, keepdims=True) for zc in zcs]
        for r, zc, var in zip(rows, zcs, variances):
            y = _silu(zc * lax.rsqrt(var + EPS) * gain + bias)
            o_ref[r, :] = (y * _silu(az_ref[r, :].astype(F32))).astype(o_ref.dtype)
        return carry

    lax.fori_loop(0, n_rows // (rb * blocks), rows_body, 0)


def _conv_seq_kernel(av_ref, ag_ref, az_ref, w_ref, cb_ref, g_ref, b_ref, o_ref,
                     pad_ref, shift_ref, conv_ref, *, seq, taps):
    half = taps // 2
    lead = CONV_LEAD
    ch = av_ref.shape[1]
    pad_ref[0:lead, :] = jnp.zeros((lead, ch), F32)
    pad_ref[lead + seq:lead + seq + lead, :] = jnp.zeros((lead, ch), F32)
    pad_ref[lead:lead + seq, :] = av_ref[...].astype(F32) * _sigmoid(ag_ref[...].astype(F32))
    rb = 128
    shifted_rows = seq + 2 * lead - SUBLANES

    def lane_block(c, carry):
        lanes = pl.ds(pl.multiple_of(c * LANES, LANES), LANES)
        for s in range(1, SUBLANES):
            shift_ref[s - 1, 0:shifted_rows, :] = pad_ref[pl.ds(s, shifted_rows), lanes]
        for r in range(seq // rb):
            acc = jnp.zeros((rb, LANES), F32)
            for j in range(taps):
                s = (lead - half + j) % SUBLANES
                base = r * rb + lead - half + j - s
                if s == 0:
                    src = pad_ref[pl.ds(base, rb), lanes]
                else:
                    src = shift_ref[s - 1, pl.ds(base, rb), :]
                acc = acc + w_ref[j:j + 1, lanes] * src
            conv_ref[r * rb:(r + 1) * rb, lanes] = acc + cb_ref[:, lanes]
        return carry

    lax.fori_loop(0, ch // LANES, lane_block, 0)
    _ln_swish_gate(conv_ref, az_ref, g_ref, b_ref, o_ref, seq)


def _conv_axial_kernel(av_ref, ag_ref, az_ref, w_ref, cb_ref, g_ref, b_ref, o_ref,
                       pad_ref, shift_ref, ver_ref, conv_ref, *, grid_h, taps):
    half = taps // 2
    lead = CONV_LEAD
    shifted_rows = GRID_W + 2 * lead - SUBLANES
    ch = av_ref.shape[1]
    hc = ch // 2
    n_tok = grid_h * GRID_W
    for r in range(grid_h):
        rows = slice(r * GRID_W, (r + 1) * GRID_W)
        ua = av_ref[rows, :].astype(F32) * _sigmoid(ag_ref[rows, :].astype(F32))
        pad_ref[r, 0:lead, :] = jnp.zeros((lead, hc), F32)
        pad_ref[r, lead + GRID_W:lead + GRID_W + lead, :] = jnp.zeros((lead, hc), F32)
        pad_ref[r, lead:lead + GRID_W, :] = ua[:, 0:hc]
        ver_ref[rows, :] = ua[:, hc:ch]

    def lane_block(c, carry):
        lanes = pl.ds(pl.multiple_of(c * LANES, LANES), LANES)
        lanes_v = pl.ds(pl.multiple_of(hc + c * LANES, LANES), LANES)

        def grid_row(r, carry2):
            out_rows = pl.ds(pl.multiple_of(r * GRID_W, GRID_W), GRID_W)
            for s in range(1, SUBLANES):
                shift_ref[s - 1] = pad_ref[r, pl.ds(s, shifted_rows), lanes]
            acc = jnp.zeros((GRID_W, LANES), F32)
            for j in range(taps):
                s = (lead - half + j) % SUBLANES
                base = lead - half + j - s
                if s == 0:
                    src = pad_ref[r, pl.ds(base, GRID_W), lanes]
                else:
                    src = shift_ref[s - 1, pl.ds(base, GRID_W), :]
                acc = acc + w_ref[j:j + 1, lanes] * src
            conv_ref[out_rows, lanes] = acc + cb_ref[:, lanes]
            return carry2

        lax.fori_loop(0, grid_h, grid_row, 0)
        for r in range(grid_h):
            acc = jnp.zeros((GRID_W, LANES), F32)
            for src in range(grid_h):
                tap = src + half - r
                acc = acc + (w_ref[tap:tap + 1, lanes_v]
                             * ver_ref[src * GRID_W:(src + 1) * GRID_W, lanes])
            conv_ref[r * GRID_W:(r + 1) * GRID_W, lanes_v] = acc + cb_ref[:, lanes_v]
        return carry

    lax.fori_loop(0, hc // LANES, lane_block, 0)
    _ln_swish_gate(conv_ref, az_ref, g_ref, b_ref, o_ref, n_tok)


def _conv_call(proj, conv_w, conv_b, ln_g, ln_b, batch, seq, col0, axial):
    taps, ch = conv_w.shape
    proj3 = proj.reshape(batch, seq, proj.shape[1])
    row = lambda v: v.reshape(1, ch)
    if axial:
        grid_h = seq // GRID_W
        assert taps // 2 >= grid_h - 1, "every grid row must lie inside the conv window"
        kern = functools.partial(_conv_axial_kernel, grid_h=grid_h, taps=taps)
        scratch = [pltpu.VMEM((grid_h, GRID_W + 2 * CONV_LEAD, ch // 2), F32),
                   pltpu.VMEM((SUBLANES - 1, GRID_W + 2 * CONV_LEAD - SUBLANES, LANES), F32),
                   pltpu.VMEM((seq, ch // 2), F32),
                   pltpu.VMEM((seq, ch), F32)]
    else:
        kern = functools.partial(_conv_seq_kernel, seq=seq, taps=taps)
        scratch = [pltpu.VMEM((seq + 2 * CONV_LEAD, ch), F32),
                   pltpu.VMEM((SUBLANES - 1, seq + 2 * CONV_LEAD - SUBLANES, LANES), F32),
                   pltpu.VMEM((seq, ch), F32)]
    assert taps // 2 < CONV_LEAD
    tok = lambda k: pl.BlockSpec((None, seq, ch), lambda b: (b, 0, col0 + k))
    const = lambda shape: pl.BlockSpec(shape, lambda b: (0, 0))
    out = pl.pallas_call(
        kern,
        grid=(batch,),
        in_specs=[tok(0), tok(1), tok(2), const((taps, ch)), const((1, ch)), const((1, ch)),
                  const((1, ch))],
        out_specs=pl.BlockSpec((None, seq, ch), lambda b: (b, 0, 0)),
        out_shape=jax.ShapeDtypeStruct((batch, seq, ch), BF16),
        scratch_shapes=scratch,
        compiler_params=_params(("parallel",)),
        name="conv_axial" if axial else "conv_seq",
    )(proj3, proj3, proj3, conv_w, row(conv_b), row(ln_g), row(ln_b))
    return out.reshape(batch * seq, ch)


def _split_bf16(x):
    hi = x.astype(BF16)
    return hi, (x - hi.astype(F32)).astype(BF16)


def _dot_split(a_parts, b_parts):
    (a_hi, a_lo), (b_hi, b_lo) = a_parts, b_parts
    return jnp.dot(jnp.concatenate([a_hi, a_lo, a_hi], axis=1),
                   jnp.concatenate([b_hi, b_hi, b_lo], axis=0), preferred_element_type=F32)


def _block_diag(parts, keep_left, keep_right):
    return tuple(jnp.concatenate([x * keep_left, x * keep_right], axis=0) for x in parts)


def _unit_triangular_inverse_pairs(n2s, eye2, keep_left, keep_right):
    ps = [eye2 - n2 for n2 in n2s]
    n_parts = [_split_bf16(n2) for n2 in n2s]
    ss = [_dot_split(parts, _block_diag(parts, keep_left, keep_right)) for parts in n_parts]
    power = 2
    while 2 * power < CHUNK:
        s_parts = [_split_bf16(s) for s in ss]
        p_parts = [_split_bf16(p) for p in ps]
        prods = [_dot_split(tuple(jnp.concatenate([pp, sp], axis=0) for pp, sp in zip(pp2, sp2)),
                            _block_diag(sp2, keep_left, keep_right))
                 for pp2, sp2 in zip(p_parts, s_parts)]
        ps = [p + pr[0:CHUNK] for p, pr in zip(ps, prods)]
        ss = [pr[CHUNK:2 * CHUNK] for pr in prods]
        power *= 2
    last = [_dot_split(_split_bf16(p), _block_diag(_split_bf16(s), keep_left, keep_right))
            for p, s in zip(ps, ss)]
    return [p + pr for p, pr in zip(ps, last)]


def _delta_kernel(*refs, seq, hb_count, chunk_unroll, zero_init):
    if zero_init:
        (q_ref, k_ref, v_ref, z_ref, ba_ref, bat_ref, wq_ref, wk_ref, wv_ref, hng_ref,
         y_ref, sout_ref, qs, ks, vs, gates, u_scr, wq_scr, lx_scr, egl_scr, o_scr, sscr) = refs
        s0_ref = None
    else:
        (q_ref, k_ref, v_ref, z_ref, ba_ref, bat_ref, wq_ref, wk_ref, wv_ref, hng_ref, s0_ref,
         y_ref, sout_ref, qs, ks, vs, gates, u_scr, wq_scr, lx_scr, egl_scr, o_scr, sscr) = refs
    nc = seq // CHUNK
    wq_rows = 2 * CHUNK
    lx_rows = CHUNK + HEAD_DIM
    h0 = pl.program_id(1) * hb_count

    row = lax.broadcasted_iota(jnp.int32, (seq, HEAD_DIM), 0)
    lane = lax.broadcasted_iota(jnp.int32, (seq, LANES), 1)

    def conv_swish(x_ref, w_ref, cols):
        x = x_ref[:, cols].astype(F32)
        prev = jnp.where(row >= 1, pltpu.roll(x, 1, 0), 0.0)
        nxt = jnp.where(row <= seq - 2, pltpu.roll(x, seq - 1, 0), 0.0)
        return _silu(w_ref[0:1, cols] * prev + w_ref[1:2, cols] * x + w_ref[2:3, cols] * nxt)

    def l2norm(x):
        return x * lax.rsqrt(jnp.sum(x * x, axis=-1, keepdims=True) + EPS)

    ba = ba_ref[...]
    for hb in range(hb_count):
        cols = slice(hb * HEAD_DIM, (hb + 1) * HEAD_DIM)
        qs[hb] = l2norm(conv_swish(q_ref, wq_ref, cols)) * (HEAD_DIM ** -0.5)
        ks[hb] = l2norm(conv_swish(k_ref, wk_ref, cols))
        vs[hb] = conv_swish(v_ref, wv_ref, cols)
        for t in range(4):
            col = jnp.sum(jnp.where(lane == t * N_HEADS + h0 + hb, ba, 0.0), axis=-1, keepdims=True)
            gates[t, hb] = jnp.broadcast_to(col, (seq, LANES))
        for d in range(2):
            if zero_init:
                sscr[d, hb] = jnp.zeros((HEAD_DIM, HEAD_DIM), F32)
            else:
                sscr[d, hb] = s0_ref[d, hb]

    ii = lax.broadcasted_iota(jnp.int32, (CHUNK, LANES), 0)
    l64 = lax.broadcasted_iota(jnp.int32, (CHUNK, LANES), 1)
    left = l64 < CHUNK
    right = l64 >= CHUNK
    jj = l64 & (CHUNK - 1)
    below = jnp.where(left, ii - jj, jj - ii)
    incl = below >= 0
    strict = below > 0
    eye2 = jnp.where(ii == jj, 1.0, 0.0).astype(F32)
    keep_left = jnp.where(left, 1.0, 0.0).astype(BF16)
    keep_right = jnp.where(left, 0.0, 1.0).astype(BF16)
    lane_t = lax.broadcasted_iota(jnp.int32, (HEAD_DIM, LANES), 1)
    left_t = lane_t < CHUNK
    right_t = lane_t >= CHUNK
    zeros = jnp.zeros((CHUNK, HEAD_DIM), F32)

    def chunk_products(problems):
        kqs = [_dot_nt(jnp.concatenate([k, q], axis=0), jnp.concatenate([k, k], axis=0))
               for q, k, *_ in problems]
        decays = []
        n2s = []
        for (q, k, v, bf, bb, gf, gb, grow2), kq in zip(problems, kqs):
            diff = jnp.where(left, gf, gb) - grow2
            decay = jnp.where(incl, jnp.exp(jnp.where(incl, diff, 0.0)), 0.0)
            decays.append(decay)
            n2s.append(jnp.where(strict, jnp.where(left, bf, bb) * kq[0:CHUNK] * decay, 0.0))
        tinvs = _unit_triangular_inverse_pairs(n2s, eye2, keep_left, keep_right)
        rhss = []
        for q, k, v, bf, bb, gf, gb, grow2 in problems:
            rhss.append(jnp.concatenate(
                [jnp.concatenate([v * bf, k * bf * jnp.exp(gf), zeros, zeros], axis=1),
                 jnp.concatenate([zeros, zeros, v * bb, k * bb * jnp.exp(gb)], axis=1)], axis=0))
        uws = [_dot(tinv, rhs) for tinv, rhs in zip(tinvs, rhss)]
        results = []
        for (q, k, v, bf, bb, gf, gb, grow2), kq, decay, uw in zip(problems, kqs, decays, uws):
            glf = gf[CHUNK - 1:CHUNK, :]
            glb = gb[0:1, :]
            kt2 = jnp.concatenate([k * jnp.exp(glf - gf), k * jnp.exp(glb - gb)], axis=0).T
            qkm = kq[CHUNK:2 * CHUNK] * decay
            out = []
            for d, (g, gl) in enumerate(((gf, glf), (gb, glb))):
                keep = left if d == 0 else right
                keep_t = left_t if d == 0 else right_t
                u = uw[:, (2 * d) * HEAD_DIM:(2 * d + 1) * HEAD_DIM]
                w = uw[:, (2 * d + 1) * HEAD_DIM:(2 * d + 2) * HEAD_DIM]
                wq = jnp.concatenate([w, q * jnp.exp(g)], axis=0).astype(BF16)
                lx = jnp.concatenate([jnp.where(keep, qkm, 0.0), jnp.where(keep_t, kt2, 0.0)],
                                     axis=0).astype(BF16)
                out.append((u, wq, lx, jnp.broadcast_to(jnp.exp(gl), (8, LANES))))
            results.append(out)
        return results

    def phase1(i, carry):
        where = []
        problems = []
        for j in range(chunk_unroll):
            c = i * chunk_unroll + j
            rows = pl.ds(pl.multiple_of(c * CHUNK, CHUNK), CHUNK)
            for hb in range(hb_count):
                where.append((c, rows, hb))
                problems.append((qs[hb, rows, :], ks[hb, rows, :], vs[hb, rows, :],
                                 gates[0, hb, rows, :], gates[1, hb, rows, :],
                                 gates[2, hb, rows, :], gates[3, hb, rows, :],
                                 bat_ref[h0 + hb, c]))
        for (c, rows, hb), res in zip(where, chunk_products(problems)):
            for d, (u, wq, lx, egl) in enumerate(res):
                u_scr[d, hb, rows, :] = u
                wq_scr[d, hb, pl.ds(pl.multiple_of(c * wq_rows, wq_rows), wq_rows), :] = wq
                lx_scr[d, hb, pl.ds(pl.multiple_of(c * lx_rows, CHUNK), lx_rows), :] = lx
                egl_scr[d, hb, c] = egl
        return carry

    lax.fori_loop(0, nc // chunk_unroll, phase1, 0)

    def phase2(c, carry):
        work = []
        for hb in range(hb_count):
            for d in range(2):
                cd = c if d == 0 else nc - 1 - c
                rows = pl.ds(pl.multiple_of(cd * CHUNK, CHUNK), CHUNK)
                work.append((hb, d, rows, sscr[d, hb], u_scr[d, hb, rows, :],
                             wq_scr[d, hb, pl.ds(pl.multiple_of(cd * wq_rows, wq_rows), wq_rows), :],
                             lx_scr[d, hb, pl.ds(pl.multiple_of(cd * lx_rows, CHUNK), lx_rows), :],
                             egl_scr[d, hb, cd]))
        wss = [jnp.dot(wq, state.astype(BF16), preferred_element_type=F32)
               for _, _, _, state, _, wq, _, _ in work]
        v_news = [(u - ws[0:CHUNK]).astype(BF16)
                  for (_, _, _, _, u, _, _, _), ws in zip(work, wss)]
        oms = [jnp.dot(lx, jnp.concatenate([v_new, v_new], axis=0), preferred_element_type=F32)
               for (_, _, _, _, _, _, lx, _), v_new in zip(work, v_news)]
        for (hb, d, rows, state, _, _, _, egl), ws, om in zip(work, wss, oms):
            o_scr[d, hb, rows, :] = ws[CHUNK:2 * CHUNK] + om[0:CHUNK]
            sscr[d, hb] = state * egl[0:1, :] + om[CHUNK:lx_rows]
        return carry

    lax.fori_loop(0, nc, phase2, 0)

    for hb in range(hb_count):
        cols = slice(hb * HEAD_DIM, (hb + 1) * HEAD_DIM)
        o = o_scr[0, hb] + o_scr[1, hb]
        o = o * lax.rsqrt(jnp.mean(o * o, axis=-1, keepdims=True) + EPS) * hng_ref[...]
        y_ref[:, cols] = (o * _silu(z_ref[:, cols].astype(F32))).astype(y_ref.dtype)
        for d in range(2):
            sout_ref[d, hb] = sscr[d, hb]


def _delta_call(proj, ba, conv_qkv_w, head_norm_g, s0, batch, seq, col_q):
    nc = seq // CHUNK
    hb_count = 4
    chunk_unroll = PHASE1_PROBLEMS // hb_count
    n_hblk = N_HEADS // hb_count
    bw = hb_count * HEAD_DIM
    kd = N_HEADS * HEAD_DIM
    proj3 = proj.reshape(batch, seq, proj.shape[1])
    ba3 = ba.reshape(batch, seq, LANES)
    g_rows = lambda lo: ba[:, lo:lo + N_HEADS].T.reshape(N_HEADS, batch, nc, 1, CHUNK)
    bat = jnp.concatenate([g_rows(2 * N_HEADS), g_rows(3 * N_HEADS)], axis=-1)
    cq = col_q // hb_count
    tok = lambda k: pl.BlockSpec((None, seq, bw), lambda b, h: (b, 0, cq + k * n_hblk + h))
    cw = lambda k: pl.BlockSpec((3, bw), lambda b, h: (0, k * n_hblk + h))
    in_specs = [tok(0), tok(1), tok(2), tok(3),
                pl.BlockSpec((None, seq, LANES), lambda b, h: (b, 0, 0)),
                pl.BlockSpec((N_HEADS, None, nc, 1, LANES), lambda b, h: (0, b, 0, 0, 0)),
                cw(0), cw(1), cw(2),
                pl.BlockSpec((1, HEAD_DIM), lambda b, h: (0, 0))]
    args = [proj3, proj3, proj3, proj3, ba3, bat, conv_qkv_w, conv_qkv_w, conv_qkv_w,
            head_norm_g.reshape(1, HEAD_DIM)]
    state_spec = pl.BlockSpec((None, 2, hb_count, HEAD_DIM, HEAD_DIM), lambda b, h: (b, 0, h, 0, 0))
    if s0 is not None:
        in_specs.append(state_spec)
        args.append(s0)
    head_scr = pltpu.VMEM((hb_count, seq, HEAD_DIM), F32)
    dir_scr = pltpu.VMEM((2, hb_count, seq, HEAD_DIM), F32)
    y, s_out = pl.pallas_call(
        functools.partial(_delta_kernel, seq=seq, hb_count=hb_count, chunk_unroll=chunk_unroll,
                          zero_init=s0 is None),
        grid=(batch, n_hblk),
        in_specs=in_specs,
        out_specs=[pl.BlockSpec((None, seq, bw), lambda b, h: (b, 0, h)), state_spec],
        out_shape=[jax.ShapeDtypeStruct((batch, seq, kd), BF16),
                   jax.ShapeDtypeStruct((batch, 2, N_HEADS, HEAD_DIM, HEAD_DIM), F32)],
        scratch_shapes=[head_scr, head_scr, head_scr,
                        pltpu.VMEM((4, hb_count, seq, LANES), F32),
                        dir_scr,
                        pltpu.VMEM((2, hb_count, nc * 2 * CHUNK, HEAD_DIM), BF16),
                        pltpu.VMEM((2, hb_count, nc * (CHUNK + HEAD_DIM), LANES), BF16),
                        pltpu.VMEM((2, hb_count, nc, 8, LANES), F32),
                        dir_scr,
                        pltpu.VMEM((2, hb_count, HEAD_DIM, HEAD_DIM), F32)],
        compiler_params=_params(("parallel", "parallel")),
        name="delta",
    )(*args)
    return y.reshape(batch * seq, kd), s_out


def _out_proj_kernel(x_ref, ya_ref, yb_ref, ga_ref, gb_ref, mod_ref, wpa_ref, wpb_ref, wo_ref,
                     fg_ref, o_ref, *, d, final):
    gate = mod_ref[:, 2 * d:3 * d]
    out_a = jnp.dot(ya_ref[...], wpa_ref[...], preferred_element_type=F32)
    out_b = jnp.dot(yb_ref[...], wpb_ref[...], preferred_element_type=F32)
    merged = (_sigmoid(ga_ref[...].astype(F32)) * out_a
              + _sigmoid(gb_ref[...].astype(F32)) * out_b)
    x = x_ref[...] + gate * jnp.dot(merged.astype(BF16), wo_ref[...], preferred_element_type=F32)
    if final:
        x = x * lax.rsqrt(jnp.mean(x * x, axis=-1, keepdims=True) + EPS) * fg_ref[...]
    o_ref[...] = x


def _out_proj_call(x, ya, yb, proj, mod, w_pa, w_pb, w_o, final_g, tokens_per_mod, final):
    n, d = x.shape
    cc = ya.shape[1]
    tm = 512
    kern = functools.partial(_out_proj_kernel, d=d, final=final)
    resident = lambda shape: pl.BlockSpec(shape, lambda i: (0, 0), pipeline_mode=pl.Buffered(1))
    return pl.pallas_call(
        kern,
        grid=(n // tm,),
        in_specs=[pl.BlockSpec((tm, d), lambda i: (i, 0)),
                  pl.BlockSpec((tm, cc), lambda i: (i, 0)),
                  pl.BlockSpec((tm, cc), lambda i: (i, 0)),
                  pl.BlockSpec((tm, d), lambda i: (i, 0)),
                  pl.BlockSpec((tm, d), lambda i: (i, 1)),
                  pl.BlockSpec((None, 1, 3 * d), lambda i: (_mod_row(i, tm, tokens_per_mod), 0, 0)),
                  resident(w_pa.shape), resident(w_pb.shape), resident(w_o.shape),
                  pl.BlockSpec((1, d), lambda i: (0, 0))],
        out_specs=pl.BlockSpec((tm, d), lambda i: (i, 0)),
        out_shape=jax.ShapeDtypeStruct((n, d), F32),
        compiler_params=_params(("parallel",)),
        name="out_proj",
    )(x, ya, yb, proj, proj, mod, w_pa, w_pb, w_o, final_g)


def kernel(x_prompt, x_sample, state_delta, c, c_ctx, w_mod, b_mod, norm_g, w_in, conv_a_w,
           conv_a_b, ln_a_g, ln_a_b, w_pa, conv_qkv_w, a_log, dt_bias, head_norm_g, w_pb, w_o,
           final_norm_g):
    batch, seq, d = x_prompt.shape
    dec_batch, dec_seq, _ = x_sample.shape
    depth = w_mod.shape[0]
    cc = conv_a_w.shape[2]
    kd = N_HEADS * HEAD_DIM
    o_ba = 3 * cc + 4 * kd
    o_gate = o_ba + 4 * N_HEADS

    cv = jnp.concatenate([c_ctx[None, :], c, jnp.zeros((MOD_ROWS - 1 - dec_batch, d), F32)], axis=0)
    mod = _mod_call(cv, w_mod, b_mod).reshape(depth, MOD_ROWS, 1, 3 * d)

    w_gate = w_in[:, :, o_gate:].astype(BF16)
    w_ba = jnp.pad(w_in[:, :, o_ba:o_gate], ((0, 0), (0, 0), (0, LANES - 4 * N_HEADS))).astype(BF16)
    gate_pad = ((0, 0), (2 * N_HEADS, LANES - 4 * N_HEADS))
    alog_rows = jnp.pad(a_log.reshape(depth, 2 * N_HEADS), gate_pad).reshape(depth, 1, LANES)
    dtb_rows = jnp.pad(dt_bias.reshape(depth, 2 * N_HEADS), gate_pad).reshape(depth, 1, LANES)
    w_pa16, w_pb16, w_o16 = w_pa.astype(BF16), w_pb.astype(BF16), w_o.astype(BF16)
    col_conv = (2 * d) // cc
    col_q = (2 * d + 3 * cc) // HEAD_DIM
    final_g = final_norm_g.reshape(1, d)

    def layer(x, l, latent):
        b, s = (dec_batch, dec_seq) if latent else (batch, seq)
        tokens_per_mod = dec_seq if latent else None
        proj, ba = _in_proj_call(x, mod[l], norm_g[l].reshape(1, d), w_in, w_gate, w_ba[l],
                                 alog_rows[l], dtb_rows[l], l, o_ba, tokens_per_mod)
        ya = _conv_call(proj, conv_a_w[l], conv_a_b[l], ln_a_g[l], ln_a_b[l], b, s, col_conv,
                        axial=latent)
        s0 = state_delta[:, l] if latent else None
        yb, s_out = _delta_call(proj, ba, conv_qkv_w[l], head_norm_g[l], s0, b, s, col_q)
        x = _out_proj_call(x, ya, yb, proj, mod[l], w_pa16[l], w_pb16[l], w_o16[l], final_g,
                           tokens_per_mod, final=(l == depth - 1))
        return x, s_out

    h = x_prompt.reshape(batch * seq, d)
    hs = x_sample.reshape(dec_batch * dec_seq, d)
    ctx_states = []
    for l in range(depth):
        h, st = layer(h, l, latent=False)
        ctx_states.append(st)
        hs, _ = layer(hs, l, latent=True)
    y_prompt = h.reshape(batch, seq, d)
    y_sample = hs.reshape(dec_batch, dec_seq, d)
    state_delta_new = jnp.stack(ctx_states, axis=1)
    return (y_prompt, y_sample, state_delta_new)
```

```python
import functools

import jax
import jax.numpy as jnp
from jax import lax
from jax.experimental import pallas as pl
from jax.experimental.pallas import tpu as pltpu

F32 = jnp.float32
BF16 = jnp.bfloat16

EPS = 1e-6
GRID_W = 64
CHUNK = 64
N_HEADS = 8
HEAD_DIM = 128
LANES = 128
SUBLANES = 8
CONV_LEAD = 16
MOD_ROWS = 8
PHASE1_PROBLEMS = 8
V7X_VMEM_LIMIT_BYTES = 56 * 1024 * 1024


def _sigmoid(x):
    return jax.nn.sigmoid(x)


def _silu(x):
    return x * jax.nn.sigmoid(x)


def _params(semantics):
    return pltpu.CompilerParams(dimension_semantics=semantics,
                                vmem_limit_bytes=V7X_VMEM_LIMIT_BYTES)


def _dot(a, b):
    return jnp.dot(a.astype(BF16), b.astype(BF16), preferred_element_type=F32)


def _dot_nt(a, b):
    return lax.dot_general(a.astype(BF16), b.astype(BF16), (((1,), (1,)), ((), ())),
                           preferred_element_type=F32)


def _mod_kernel(cv_ref, w_ref, b_ref, o_ref):
    a = _silu(cv_ref[...])
    o_ref[...] = _dot(a, w_ref[...]) + b_ref[...]


def _mod_call(cv, w_mod, b_mod):
    depth, d, d3 = w_mod.shape
    tn = 1024
    return pl.pallas_call(
        _mod_kernel,
        grid=(depth, d3 // tn),
        in_specs=[pl.BlockSpec((MOD_ROWS, d), lambda l, j: (0, 0)),
                  pl.BlockSpec((None, d, tn), lambda l, j: (l, 0, j)),
                  pl.BlockSpec((None, 1, tn), lambda l, j: (l, 0, j))],
        out_specs=pl.BlockSpec((None, MOD_ROWS, tn), lambda l, j: (l, 0, j)),
        out_shape=jax.ShapeDtypeStruct((depth, MOD_ROWS, d3), F32),
        compiler_params=_params(("parallel", "parallel")),
        name="mod",
    )(cv, w_mod, b_mod.reshape(depth, 1, d3))


def _mod_row(i, tm, tokens_per_mod):
    if tokens_per_mod is None:
        return 0
    return 1 + (i * tm) // tokens_per_mod


def _in_proj_kernel(x_ref, mod_ref, ng_ref, w_ref, wba_ref, alog_ref, dtb_ref,
                    proj_ref, ba_ref, hn_ref, wbf_ref, *, tm, d):
    j = pl.program_id(0)
    i = pl.program_id(1)
    base = i * tm
    tile = pl.ds(pl.multiple_of(base, tm), tm)

    @pl.when(j == 0)
    def _():
        shift = mod_ref[:, 0:d]
        scale = mod_ref[:, d:2 * d]
        gain = ng_ref[...] * (1.0 + scale)
        rb = 64

        def norm_rows(t, carry):
            x = x_ref[pl.ds(pl.multiple_of(t * rb, rb), rb), :]
            inv = lax.rsqrt(jnp.mean(x * x, axis=-1, keepdims=True) + EPS)
            hn_ref[pl.ds(pl.multiple_of(base + t * rb, rb), rb), :] = (
                x * inv * gain + shift).astype(BF16)
            return carry

        lax.fori_loop(0, tm // rb, norm_rows, 0)

        pba = _dot_nt(hn_ref[tile, :], wba_ref[0])
        beta = _sigmoid(pba)
        z = pba + dtb_ref[...]
        softplus = jnp.maximum(z, 0.0) + jnp.log(1.0 + jnp.exp(-jnp.abs(z)))
        g = -jnp.exp(alog_ref[...]) * softplus
        pos = lax.broadcasted_iota(jnp.int32, (tm, LANES), 0) & (CHUNK - 1)
        pre = g
        suf = g
        s = 1
        while s < CHUNK:
            pre = pre + jnp.where(pos >= s, pltpu.roll(pre, s, 0), 0.0)
            suf = suf + jnp.where(pos < CHUNK - s, pltpu.roll(suf, tm - s, 0), 0.0)
            s *= 2
        lane = lax.broadcasted_iota(jnp.int32, (tm, LANES), 1)
        ba_ref[...] = jnp.where(lane < 2 * N_HEADS, beta,
                                jnp.where(lane < 3 * N_HEADS, pre,
                                          jnp.where(lane < 4 * N_HEADS, suf, 0.0)))

    @pl.when(i == 0)
    def _():
        wbf_ref[...] = w_ref[0].astype(BF16)

    proj_ref[...] = _dot_nt(hn_ref[tile, :], wbf_ref[...]).astype(proj_ref.dtype)


def _in_proj_call(x, mod, norm_g, w_t, alog_row, dtb_row, layer, main_width, skip, tokens_per_mod):
    n, d = x.shape
    gate_start = main_width + skip
    gate_width = w_t.shape[1] - gate_start
    tm, tn = 1024, 512
    n_tok = n // tm
    n_main = main_width // tn
    n_gate = gate_width // tn
    kern = functools.partial(_in_proj_kernel, tm=tm, d=d)
    tok_tile = lambda j, i: jnp.where(j == 0, i, n_tok - 1)
    assert gate_start % SUBLANES == 0
    w_row = lambda j: pl.multiple_of(
        jnp.where(j < n_main, j * tn, gate_start + (j - n_main) * tn), SUBLANES)
    return pl.pallas_call(
        kern,
        grid=(n_main + n_gate, n_tok),
        in_specs=[pl.BlockSpec((tm, d), lambda j, i: (tok_tile(j, i), 0)),
                  pl.BlockSpec((None, 1, 3 * d),
                               lambda j, i: (_mod_row(tok_tile(j, i), tm, tokens_per_mod), 0, 0)),
                  pl.BlockSpec((1, d), lambda j, i: (0, 0)),
                  pl.BlockSpec((pl.Element(1), pl.Element(tn), pl.Element(d)),
                               lambda j, i: (layer, w_row(j), 0)),
                  pl.BlockSpec((pl.Element(1), pl.Element(LANES), pl.Element(d)),
                               lambda j, i: (layer, main_width, 0)),
                  pl.BlockSpec((1, LANES), lambda j, i: (0, 0)),
                  pl.BlockSpec((1, LANES), lambda j, i: (0, 0))],
        out_specs=[pl.BlockSpec((tm, tn),
                                lambda j, i: (i, jnp.where(j < n_main, j + n_gate, j - n_main))),
                   pl.BlockSpec((tm, LANES), lambda j, i: (tok_tile(j, i), 0))],
        out_shape=[jax.ShapeDtypeStruct((n, gate_width + main_width), BF16),
                   jax.ShapeDtypeStruct((n, LANES), F32)],
        scratch_shapes=[pltpu.VMEM((n, d), BF16), pltpu.VMEM((tn, d), BF16)],
        compiler_params=_params(("arbitrary", "arbitrary")),
        name="in_proj",
    )(x, mod, norm_g, w_t, w_t, alog_row, dtb_row)


def _ln_swish_gate(conv_ref, az_ref, g_ref, b_ref, o_ref, n_rows):
    rb = 16
    blocks = 4
    gain = g_ref[...]
    bias = b_ref[...]

    def rows_body(t, carry):
        rows = [pl.ds(pl.multiple_of((t * blocks + u) * rb, rb), rb) for u in range(blocks)]
        zs = [conv_ref[r, :] for r in rows]
        mus = [jnp.mean(z, axis=-1, keepdims=True) for z in zs]
        zcs = [z - mu for z, mu in zip(zs, mus)]
        variances = [jnp.mean(zc * zc, axis=-1, keepdims=True) for zc in zcs]
        for r, zc, var in zip(rows, zcs, variances):
            y = _silu(zc * lax.rsqrt(var + EPS) * gain + bias)
            o_ref[r, :] = (y * _silu(az_ref[r, :].astype(F32))).astype(o_ref.dtype)
        return carry

    lax.fori_loop(0, n_rows // (rb * blocks), rows_body, 0)


def _conv_seq_kernel(av_ref, ag_ref, az_ref, w_ref, cb_ref, g_ref, b_ref, o_ref,
                     pad_ref, shift_ref, conv_ref, *, seq, taps):
    half = taps // 2
    lead = CONV_LEAD
    ch = av_ref.shape[1]
    pad_ref[0:lead, :] = jnp.zeros((lead, ch), F32)
    pad_ref[lead + seq:lead + seq + lead, :] = jnp.zeros((lead, ch), F32)
    pad_ref[lead:lead + seq, :] = av_ref[...].astype(F32) * _sigmoid(ag_ref[...].astype(F32))
    rb = 128
    shifted_rows = seq + 2 * lead - SUBLANES

    def lane_block(c, carry):
        lanes = pl.ds(pl.multiple_of(c * LANES, LANES), LANES)
        for s in range(1, SUBLANES):
            shift_ref[s - 1, 0:shifted_rows, :] = pad_ref[pl.ds(s, shifted_rows), lanes]
        for r in range(seq // rb):
            acc = jnp.zeros((rb, LANES), F32)
            for j in range(taps):
                s = (lead - half + j) % SUBLANES
                base = r * rb + lead - half + j - s
                if s == 0:
                    src = pad_ref[pl.ds(base, rb), lanes]
                else:
                    src = shift_ref[s - 1, pl.ds(base, rb), :]
                acc = acc + w_ref[j:j + 1, lanes] * src
            conv_ref[r * rb:(r + 1) * rb, lanes] = acc + cb_ref[:, lanes]
        return carry

    lax.fori_loop(0, ch // LANES, lane_block, 0)
    _ln_swish_gate(conv_ref, az_ref, g_ref, b_ref, o_ref, seq)


def _conv_axial_kernel(av_ref, ag_ref, az_ref, w_ref, cb_ref, g_ref, b_ref, o_ref,
                       pad_ref, shift_ref, ver_ref, conv_ref, *, grid_h, taps):
    half = taps // 2
    lead = CONV_LEAD
    shifted_rows = GRID_W + 2 * lead - SUBLANES
    ch = av_ref.shape[1]
    hc = ch // 2
    n_tok = grid_h * GRID_W
    for r in range(grid_h):
        rows = slice(r * GRID_W, (r + 1) * GRID_W)
        ua = av_ref[rows, :].astype(F32) * _sigmoid(ag_ref[rows, :].astype(F32))
        pad_ref[r, 0:lead, :] = jnp.zeros((lead, hc), F32)
        pad_ref[r, lead + GRID_W:lead + GRID_W + lead, :] = jnp.zeros((lead, hc), F32)
        pad_ref[r, lead:lead + GRID_W, :] = ua[:, 0:hc]
        ver_ref[rows, :] = ua[:, hc:ch]

    def lane_block(c, carry):
        lanes = pl.ds(pl.multiple_of(c * LANES, LANES), LANES)
        lanes_v = pl.ds(pl.multiple_of(hc + c * LANES, LANES), LANES)

        def grid_row(r, carry2):
            out_rows = pl.ds(pl.multiple_of(r * GRID_W, GRID_W), GRID_W)
            for s in range(1, SUBLANES):
                shift_ref[s - 1] = pad_ref[r, pl.ds(s, shifted_rows), lanes]
            acc = jnp.zeros((GRID_W, LANES), F32)
            for j in range(taps):
                s = (lead - half + j) % SUBLANES
                base = lead - half + j - s
                if s == 0:
                    src = pad_ref[r, pl.ds(base, GRID_W), lanes]
                else:
                    src = shift_ref[s - 1, pl.ds(base, GRID_W), :]
                acc = acc + w_ref[j:j + 1, lanes] * src
            conv_ref[out_rows, lanes] = acc + cb_ref[:, lanes]
            return carry2

        lax.fori_loop(0, grid_h, grid_row, 0)
        for r in range(grid_h):
            acc = jnp.zeros((GRID_W, LANES), F32)
            for src in range(grid_h):
                tap = src + half - r
                acc = acc + (w_ref[tap:tap + 1, lanes_v]
                             * ver_ref[src * GRID_W:(src + 1) * GRID_W, lanes])
            conv_ref[r * GRID_W:(r + 1) * GRID_W, lanes_v] = acc + cb_ref[:, lanes_v]
        return carry

    lax.fori_loop(0, hc // LANES, lane_block, 0)
    _ln_swish_gate(conv_ref, az_ref, g_ref, b_ref, o_ref, n_tok)


def _conv_call(proj, conv_w, conv_b, ln_g, ln_b, batch, seq, col0, axial):
    taps, ch = conv_w.shape
    proj3 = proj.reshape(batch, seq, proj.shape[1])
    row = lambda v: v.reshape(1, ch)
    if axial:
        grid_h = seq // GRID_W
        assert taps // 2 >= grid_h - 1, "every grid row must lie inside the conv window"
        kern = functools.partial(_conv_axial_kernel, grid_h=grid_h, taps=taps)
        scratch = [pltpu.VMEM((grid_h, GRID_W + 2 * CONV_LEAD, ch // 2), F32),
                   pltpu.VMEM((SUBLANES - 1, GRID_W + 2 * CONV_LEAD - SUBLANES, LANES), F32),
                   pltpu.VMEM((seq, ch // 2), F32),
                   pltpu.VMEM((seq, ch), F32)]
    else:
        kern = functools.partial(_conv_seq_kernel, seq=seq, taps=taps)
        scratch = [pltpu.VMEM((seq + 2 * CONV_LEAD, ch), F32),
                   pltpu.VMEM((SUBLANES - 1, seq + 2 * CONV_LEAD - SUBLANES, LANES), F32),
                   pltpu.VMEM((seq, ch), F32)]
    assert taps // 2 < CONV_LEAD
    tok = lambda k: pl.BlockSpec((None, seq, ch), lambda b: (b, 0, col0 + k))
    const = lambda shape: pl.BlockSpec(shape, lambda b: (0, 0))
    out = pl.pallas_call(
        kern,
        grid=(batch,),
        in_specs=[tok(0), tok(1), tok(2), const((taps, ch)), const((1, ch)), const((1, ch)),
                  const((1, ch))],
        out_specs=pl.BlockSpec((None, seq, ch), lambda b: (b, 0, 0)),
        out_shape=jax.ShapeDtypeStruct((batch, seq, ch), BF16),
        scratch_shapes=scratch,
        compiler_params=_params(("parallel",)),
        name="conv_axial" if axial else "conv_seq",
    )(proj3, proj3, proj3, conv_w, row(conv_b), row(ln_g), row(ln_b))
    return out.reshape(batch * seq, ch)


def _split_bf16(x):
    hi = x.astype(BF16)
    return hi, (x - hi.astype(F32)).astype(BF16)


def _dot_split(a_parts, b_parts):
    (a_hi, a_lo), (b_hi, b_lo) = a_parts, b_parts
    n = b_hi.shape[1]
    rhs = jnp.concatenate([jnp.concatenate([b_hi, b_lo], axis=1),
                           jnp.concatenate([b_hi, jnp.zeros_like(b_lo)], axis=1)], axis=0)
    both = jnp.dot(jnp.concatenate([a_hi, a_lo], axis=1), rhs, preferred_element_type=F32)
    return both[:, 0:n] + both[:, n:2 * n]


def _block_diag(parts, keep_left, keep_right):
    return tuple(jnp.concatenate([x * keep_left, x * keep_right], axis=0) for x in parts)


def _unit_triangular_inverse_pairs(n2s, eye2, keep_left, keep_right):
    ps = [eye2 - n2 for n2 in n2s]
    n_parts = [_split_bf16(n2) for n2 in n2s]
    ss = [_dot_split(parts, _block_diag(parts, keep_left, keep_right)) for parts in n_parts]
    power = 2
    while 2 * power < CHUNK:
        s_parts = [_split_bf16(s) for s in ss]
        p_parts = [_split_bf16(p) for p in ps]
        prods = [_dot_split(tuple(jnp.concatenate([pp, sp], axis=0) for pp, sp in zip(pp2, sp2)),
                            _block_diag(sp2, keep_left, keep_right))
                 for pp2, sp2 in zip(p_parts, s_parts)]
        ps = [p + pr[0:CHUNK] for p, pr in zip(ps, prods)]
        ss = [pr[CHUNK:2 * CHUNK] for pr in prods]
        power *= 2
    last = [_dot_split(_split_bf16(p), _block_diag(_split_bf16(s), keep_left, keep_right))
            for p, s in zip(ps, ss)]
    return [p + pr for p, pr in zip(ps, last)]


def _delta_kernel(*refs, seq, hb_count, chunk_unroll, zero_init):
    if zero_init:
        (q_ref, k_ref, v_ref, z_ref, ba_ref, bat_ref, wq_ref, wk_ref, wv_ref, hng_ref,
         y_ref, sout_ref, qs, ks, vs, gates, u_scr, wq_scr, lx_scr, egl_scr, o_scr, sscr) = refs
        s0_ref = None
    else:
        (q_ref, k_ref, v_ref, z_ref, ba_ref, bat_ref, wq_ref, wk_ref, wv_ref, hng_ref, s0_ref,
         y_ref, sout_ref, qs, ks, vs, gates, u_scr, wq_scr, lx_scr, egl_scr, o_scr, sscr) = refs
    nc = seq // CHUNK
    wq_rows = 2 * CHUNK
    lx_rows = CHUNK + HEAD_DIM
    h0 = pl.program_id(1) * hb_count

    row = lax.broadcasted_iota(jnp.int32, (seq, HEAD_DIM), 0)
    lane = lax.broadcasted_iota(jnp.int32, (seq, LANES), 1)

    def conv_swish(x_ref, w_ref, cols):
        x = x_ref[:, cols].astype(F32)
        prev = jnp.where(row >= 1, pltpu.roll(x, 1, 0), 0.0)
        nxt = jnp.where(row <= seq - 2, pltpu.roll(x, seq - 1, 0), 0.0)
        return _silu(w_ref[0:1, cols] * prev + w_ref[1:2, cols] * x + w_ref[2:3, cols] * nxt)

    def l2norm(x):
        return x * lax.rsqrt(jnp.sum(x * x, axis=-1, keepdims=True) + EPS)

    ba = ba_ref[...]
    for hb in range(hb_count):
        cols = slice(hb * HEAD_DIM, (hb + 1) * HEAD_DIM)
        qs[hb] = l2norm(conv_swish(q_ref, wq_ref, cols)) * (HEAD_DIM ** -0.5)
        ks[hb] = l2norm(conv_swish(k_ref, wk_ref, cols))
        vs[hb] = conv_swish(v_ref, wv_ref, cols)
        for t in range(4):
            col = jnp.sum(jnp.where(lane == t * N_HEADS + h0 + hb, ba, 0.0), axis=-1, keepdims=True)
            gates[t, hb] = jnp.broadcast_to(col, (seq, LANES))
        for d in range(2):
            if zero_init:
                sscr[d, hb] = jnp.zeros((HEAD_DIM, HEAD_DIM), F32)
            else:
                sscr[d, hb] = s0_ref[d, hb]

    ii = lax.broadcasted_iota(jnp.int32, (CHUNK, LANES), 0)
    l64 = lax.broadcasted_iota(jnp.int32, (CHUNK, LANES), 1)
    left = l64 < CHUNK
    right = l64 >= CHUNK
    jj = l64 & (CHUNK - 1)
    below = jnp.where(left, ii - jj, jj - ii)
    incl = below >= 0
    strict = below > 0
    eye2 = jnp.where(ii == jj, 1.0, 0.0).astype(F32)
    keep_left = jnp.where(left, 1.0, 0.0).astype(BF16)
    keep_right = jnp.where(left, 0.0, 1.0).astype(BF16)
    lane_t = lax.broadcasted_iota(jnp.int32, (HEAD_DIM, LANES), 1)
    left_t = lane_t < CHUNK
    right_t = lane_t >= CHUNK
    zeros = jnp.zeros((CHUNK, HEAD_DIM), F32)

    def chunk_products(problems):
        kqs = [_dot_nt(jnp.concatenate([k, q], axis=0), jnp.concatenate([k, k], axis=0))
               for q, k, *_ in problems]
        decays = []
        n2s = []
        for (q, k, v, bf, bb, gf, gb, grow2), kq in zip(problems, kqs):
            diff = jnp.where(left, gf, gb) - grow2
            decay = jnp.where(incl, jnp.exp(jnp.where(incl, diff, 0.0)), 0.0)
            decays.append(decay)
            n2s.append(jnp.where(strict, jnp.where(left, bf, bb) * kq[0:CHUNK] * decay, 0.0))
        tinvs = _unit_triangular_inverse_pairs(n2s, eye2, keep_left, keep_right)
        rhss = []
        for q, k, v, bf, bb, gf, gb, grow2 in problems:
            rhss.append(jnp.concatenate(
                [jnp.concatenate([v * bf, k * bf * jnp.exp(gf), zeros, zeros], axis=1),
                 jnp.concatenate([zeros, zeros, v * bb, k * bb * jnp.exp(gb)], axis=1)], axis=0))
        uws = [_dot(tinv, rhs) for tinv, rhs in zip(tinvs, rhss)]
        results = []
        for (q, k, v, bf, bb, gf, gb, grow2), kq, decay, uw in zip(problems, kqs, decays, uws):
            glf = gf[CHUNK - 1:CHUNK, :]
            glb = gb[0:1, :]
            kt2 = jnp.concatenate([k * jnp.exp(glf - gf), k * jnp.exp(glb - gb)], axis=0).T
            qkm = kq[CHUNK:2 * CHUNK] * decay
            out = []
            for d, (g, gl) in enumerate(((gf, glf), (gb, glb))):
                keep = left if d == 0 else right
                keep_t = left_t if d == 0 else right_t
                u = uw[:, (2 * d) * HEAD_DIM:(2 * d + 1) * HEAD_DIM]
                w = uw[:, (2 * d + 1) * HEAD_DIM:(2 * d + 2) * HEAD_DIM]
                wq = jnp.concatenate([w, q * jnp.exp(g)], axis=0).astype(BF16)
                lx = jnp.concatenate([jnp.where(keep, qkm, 0.0), jnp.where(keep_t, kt2, 0.0)],
                                     axis=0).astype(BF16)
                out.append((u, wq, lx, jnp.broadcast_to(jnp.exp(gl), (8, LANES))))
            results.append(out)
        return results

    def phase1(i, carry):
        where = []
        problems = []
        for j in range(chunk_unroll):
            c = i * chunk_unroll + j
            rows = pl.ds(pl.multiple_of(c * CHUNK, CHUNK), CHUNK)
            for hb in range(hb_count):
                where.append((c, rows, hb))
                problems.append((qs[hb, rows, :], ks[hb, rows, :], vs[hb, rows, :],
                                 gates[0, hb, rows, :], gates[1, hb, rows, :],
                                 gates[2, hb, rows, :], gates[3, hb, rows, :],
                                 bat_ref[h0 + hb, c]))
        for (c, rows, hb), res in zip(where, chunk_products(problems)):
            for d, (u, wq, lx, egl) in enumerate(res):
                u_scr[d, hb, rows, :] = u
                wq_scr[d, hb, pl.ds(pl.multiple_of(c * wq_rows, wq_rows), wq_rows), :] = wq
                lx_scr[d, hb, pl.ds(pl.multiple_of(c * lx_rows, CHUNK), lx_rows), :] = lx
                egl_scr[d, hb, c] = egl
        return carry

    lax.fori_loop(0, nc // chunk_unroll, phase1, 0)

    def phase2(c, carry):
        work = []
        for hb in range(hb_count):
            for d in range(2):
                cd = c if d == 0 else nc - 1 - c
                rows = pl.ds(pl.multiple_of(cd * CHUNK, CHUNK), CHUNK)
                work.append((hb, d, rows, sscr[d, hb], u_scr[d, hb, rows, :],
                             wq_scr[d, hb, pl.ds(pl.multiple_of(cd * wq_rows, wq_rows), wq_rows), :],
                             lx_scr[d, hb, pl.ds(pl.multiple_of(cd * lx_rows, CHUNK), lx_rows), :],
                             egl_scr[d, hb, cd]))
        wss = [jnp.dot(wq, state.astype(BF16), preferred_element_type=F32)
               for _, _, _, state, _, wq, _, _ in work]
        v_news = [(u - ws[0:CHUNK]).astype(BF16)
                  for (_, _, _, _, u, _, _, _), ws in zip(work, wss)]
        oms = [jnp.dot(lx, jnp.concatenate([v_new, v_new], axis=0), preferred_element_type=F32)
               for (_, _, _, _, _, _, lx, _), v_new in zip(work, v_news)]
        for (hb, d, rows, state, _, _, _, egl), ws, om in zip(work, wss, oms):
            o_scr[d, hb, rows, :] = ws[CHUNK:2 * CHUNK] + om[0:CHUNK]
            sscr[d, hb] = state * egl[0:1, :] + om[CHUNK:lx_rows]
        return carry

    lax.fori_loop(0, nc, phase2, 0)

    for hb in range(hb_count):
        cols = slice(hb * HEAD_DIM, (hb + 1) * HEAD_DIM)
        o = o_scr[0, hb] + o_scr[1, hb]
        o = o * lax.rsqrt(jnp.mean(o * o, axis=-1, keepdims=True) + EPS) * hng_ref[...]
        y_ref[:, cols] = (o * _silu(z_ref[:, cols].astype(F32))).astype(y_ref.dtype)
        for d in range(2):
            sout_ref[d, hb] = sscr[d, hb]


def _delta_call(proj, ba, conv_qkv_w, head_norm_g, s0, batch, seq, col_q):
    nc = seq // CHUNK
    hb_count = 4
    chunk_unroll = PHASE1_PROBLEMS // hb_count
    n_hblk = N_HEADS // hb_count
    bw = hb_count * HEAD_DIM
    kd = N_HEADS * HEAD_DIM
    proj3 = proj.reshape(batch, seq, proj.shape[1])
    ba3 = ba.reshape(batch, seq, LANES)
    g_rows = lambda lo: ba[:, lo:lo + N_HEADS].T.reshape(N_HEADS, batch, nc, 1, CHUNK)
    bat = jnp.concatenate([g_rows(2 * N_HEADS), g_rows(3 * N_HEADS)], axis=-1)
    cq = col_q // hb_count
    tok = lambda k: pl.BlockSpec((None, seq, bw), lambda b, h: (b, 0, cq + k * n_hblk + h))
    cw = lambda k: pl.BlockSpec((3, bw), lambda b, h: (0, k * n_hblk + h))
    in_specs = [tok(0), tok(1), tok(2), tok(3),
                pl.BlockSpec((None, seq, LANES), lambda b, h: (b, 0, 0)),
                pl.BlockSpec((N_HEADS, None, nc, 1, LANES), lambda b, h: (0, b, 0, 0, 0)),
                cw(0), cw(1), cw(2),
                pl.BlockSpec((1, HEAD_DIM), lambda b, h: (0, 0))]
    args = [proj3, proj3, proj3, proj3, ba3, bat, conv_qkv_w, conv_qkv_w, conv_qkv_w,
            head_norm_g.reshape(1, HEAD_DIM)]
    state_spec = pl.BlockSpec((None, 2, hb_count, HEAD_DIM, HEAD_DIM), lambda b, h: (b, 0, h, 0, 0))
    if s0 is not None:
        in_specs.append(state_spec)
        args.append(s0)
    head_scr = pltpu.VMEM((hb_count, seq, HEAD_DIM), F32)
    dir_scr = pltpu.VMEM((2, hb_count, seq, HEAD_DIM), F32)
    y, s_out = pl.pallas_call(
        functools.partial(_delta_kernel, seq=seq, hb_count=hb_count, chunk_unroll=chunk_unroll,
                          zero_init=s0 is None),
        grid=(batch, n_hblk),
        in_specs=in_specs,
        out_specs=[pl.BlockSpec((None, seq, bw), lambda b, h: (b, 0, h)), state_spec],
        out_shape=[jax.ShapeDtypeStruct((batch, seq, kd), BF16),
                   jax.ShapeDtypeStruct((batch, 2, N_HEADS, HEAD_DIM, HEAD_DIM), F32)],
        scratch_shapes=[head_scr, head_scr, head_scr,
                        pltpu.VMEM((4, hb_count, seq, LANES), F32),
                        dir_scr,
                        pltpu.VMEM((2, hb_count, nc * 2 * CHUNK, HEAD_DIM), BF16),
                        pltpu.VMEM((2, hb_count, nc * (CHUNK + HEAD_DIM), LANES), BF16),
                        pltpu.VMEM((2, hb_count, nc, 8, LANES), F32),
                        dir_scr,
                        pltpu.VMEM((2, hb_count, HEAD_DIM, HEAD_DIM), F32)],
        compiler_params=_params(("parallel", "parallel")),
        name="delta",
    )(*args)
    return y.reshape(batch * seq, kd), s_out


def _out_proj_kernel(x_ref, ya_ref, yb_ref, ga_ref, gb_ref, mod_ref, wpa_ref, wpb_ref, wo_ref,
                     fg_ref, o_ref, *, d, final):
    gate = mod_ref[:, 2 * d:3 * d]
    out_a = jnp.dot(ya_ref[...], wpa_ref[...], preferred_element_type=F32)
    out_b = jnp.dot(yb_ref[...], wpb_ref[...], preferred_element_type=F32)
    merged = (_sigmoid(ga_ref[...].astype(F32)) * out_a
              + _sigmoid(gb_ref[...].astype(F32)) * out_b)
    x = x_ref[...] + gate * jnp.dot(merged.astype(BF16), wo_ref[...], preferred_element_type=F32)
    if final:
        x = x * lax.rsqrt(jnp.mean(x * x, axis=-1, keepdims=True) + EPS) * fg_ref[...]
    o_ref[...] = x


def _out_proj_call(x, ya, yb, proj, mod, w_pa, w_pb, w_o, final_g, tokens_per_mod, final):
    n, d = x.shape
    cc = ya.shape[1]
    tm = 512
    kern = functools.partial(_out_proj_kernel, d=d, final=final)
    resident = lambda shape: pl.BlockSpec(shape, lambda i: (0, 0), pipeline_mode=pl.Buffered(1))
    return pl.pallas_call(
        kern,
        grid=(n // tm,),
        in_specs=[pl.BlockSpec((tm, d), lambda i: (i, 0)),
                  pl.BlockSpec((tm, cc), lambda i: (i, 0)),
                  pl.BlockSpec((tm, cc), lambda i: (i, 0)),
                  pl.BlockSpec((tm, d), lambda i: (i, 0)),
                  pl.BlockSpec((tm, d), lambda i: (i, 1)),
                  pl.BlockSpec((None, 1, 3 * d), lambda i: (_mod_row(i, tm, tokens_per_mod), 0, 0)),
                  resident(w_pa.shape), resident(w_pb.shape), resident(w_o.shape),
                  pl.BlockSpec((1, d), lambda i: (0, 0))],
        out_specs=pl.BlockSpec((tm, d), lambda i: (i, 0)),
        out_shape=jax.ShapeDtypeStruct((n, d), F32),
        compiler_params=_params(("parallel",)),
        name="out_proj",
    )(x, ya, yb, proj, proj, mod, w_pa, w_pb, w_o, final_g)


def kernel(x_prompt, x_sample, state_delta, c, c_ctx, w_mod, b_mod, norm_g, w_in, conv_a_w,
           conv_a_b, ln_a_g, ln_a_b, w_pa, conv_qkv_w, a_log, dt_bias, head_norm_g, w_pb, w_o,
           final_norm_g):
    batch, seq, d = x_prompt.shape
    dec_batch, dec_seq, _ = x_sample.shape
    depth = w_mod.shape[0]
    cc = conv_a_w.shape[2]
    kd = N_HEADS * HEAD_DIM
    o_ba = 3 * cc + 4 * kd
    o_gate = o_ba + 4 * N_HEADS

    cv = jnp.concatenate([c_ctx[None, :], c, jnp.zeros((MOD_ROWS - 1 - dec_batch, d), F32)], axis=0)
    mod = _mod_call(cv, w_mod, b_mod).reshape(depth, MOD_ROWS, 1, 3 * d)

    w_t = jnp.swapaxes(w_in, 1, 2)
    gate_pad = ((0, 0), (2 * N_HEADS, LANES - 4 * N_HEADS))
    alog_rows = jnp.pad(a_log.reshape(depth, 2 * N_HEADS), gate_pad).reshape(depth, 1, LANES)
    dtb_rows = jnp.pad(dt_bias.reshape(depth, 2 * N_HEADS), gate_pad).reshape(depth, 1, LANES)
    w_pa16, w_pb16, w_o16 = w_pa.astype(BF16), w_pb.astype(BF16), w_o.astype(BF16)
    col_conv = (2 * d) // cc
    col_q = (2 * d + 3 * cc) // HEAD_DIM
    final_g = final_norm_g.reshape(1, d)

    def layer(x, l, latent):
        b, s = (dec_batch, dec_seq) if latent else (batch, seq)
        tokens_per_mod = dec_seq if latent else None
        proj, ba = _in_proj_call(x, mod[l], norm_g[l].reshape(1, d), w_t, alog_rows[l],
                                 dtb_rows[l], l, o_ba, o_gate - o_ba, tokens_per_mod)
        ya = _conv_call(proj, conv_a_w[l], conv_a_b[l], ln_a_g[l], ln_a_b[l], b, s, col_conv,
                        axial=latent)
        s0 = state_delta[:, l] if latent else None
        yb, s_out = _delta_call(proj, ba, conv_qkv_w[l], head_norm_g[l], s0, b, s, col_q)
        x = _out_proj_call(x, ya, yb, proj, mod[l], w_pa16[l], w_pb16[l], w_o16[l], final_g,
                           tokens_per_mod, final=(l == depth - 1))
        return x, s_out

    h = x_prompt.reshape(batch * seq, d)
    hs = x_sample.reshape(dec_batch * dec_seq, d)
    ctx_states = []
    for l in range(depth):
        h, st = layer(h, l, latent=False)
        ctx_states.append(st)
        hs, _ = layer(hs, l, latent=True)
    y_prompt = h.reshape(batch, seq, d)
    y_sample = hs.reshape(dec_batch, dec_seq, d)
    state_delta_new = jnp.stack(ctx_states, axis=1)
    return (y_prompt, y_sample, state_delta_new)
```

```python
import functools

import jax
import jax.numpy as jnp
from jax import lax
from jax.experimental import pallas as pl
from jax.experimental.pallas import tpu as pltpu

F32 = jnp.float32
BF16 = jnp.bfloat16

EPS = 1e-6
GRID_W = 64
CHUNK = 64
N_HEADS = 8
HEAD_DIM = 128
LANES = 128
SUBLANES = 8
CONV_LEAD = 16
MOD_ROWS = 8
PHASE1_PROBLEMS = 16
V7X_VMEM_LIMIT_BYTES = 56 * 1024 * 1024


def _sigmoid(x):
    return jax.nn.sigmoid(x)


def _silu(x):
    return x * jax.nn.sigmoid(x)


def _params(semantics):
    return pltpu.CompilerParams(dimension_semantics=semantics,
                                vmem_limit_bytes=V7X_VMEM_LIMIT_BYTES)


def _dot(a, b):
    return jnp.dot(a.astype(BF16), b.astype(BF16), preferred_element_type=F32)


def _dot_nt(a, b):
    return lax.dot_general(a.astype(BF16), b.astype(BF16), (((1,), (1,)), ((), ())),
                           preferred_element_type=F32)


def _mod_kernel(cv_ref, w_ref, b_ref, o_ref):
    a = _silu(cv_ref[...])
    o_ref[...] = _dot(a, w_ref[...]) + b_ref[...]


def _mod_call(cv, w_mod, b_mod):
    depth, d, d3 = w_mod.shape
    tn = 1024
    return pl.pallas_call(
        _mod_kernel,
        grid=(depth, d3 // tn),
        in_specs=[pl.BlockSpec((MOD_ROWS, d), lambda l, j: (0, 0)),
                  pl.BlockSpec((None, d, tn), lambda l, j: (l, 0, j)),
                  pl.BlockSpec((None, 1, tn), lambda l, j: (l, 0, j))],
        out_specs=pl.BlockSpec((None, MOD_ROWS, tn), lambda l, j: (l, 0, j)),
        out_shape=jax.ShapeDtypeStruct((depth, MOD_ROWS, d3), F32),
        compiler_params=_params(("parallel", "parallel")),
        name="mod",
    )(cv, w_mod, b_mod.reshape(depth, 1, d3))


def _mod_row(i, tm, tokens_per_mod):
    if tokens_per_mod is None:
        return 0
    return 1 + (i * tm) // tokens_per_mod


def _in_proj_kernel(x_ref, mod_ref, ng_ref, w_ref, wba_ref, alog_ref, dtb_ref,
                    proj_ref, ba_ref, hn_ref, wbf_ref, *, tm, d):
    j = pl.program_id(0)
    i = pl.program_id(1)
    base = i * tm
    tile = pl.ds(pl.multiple_of(base, tm), tm)

    @pl.when(j == 0)
    def _():
        shift = mod_ref[:, 0:d]
        scale = mod_ref[:, d:2 * d]
        gain = ng_ref[...] * (1.0 + scale)
        rb = 64

        def norm_rows(t, carry):
            x = x_ref[pl.ds(pl.multiple_of(t * rb, rb), rb), :]
            inv = lax.rsqrt(jnp.mean(x * x, axis=-1, keepdims=True) + EPS)
            hn_ref[pl.ds(pl.multiple_of(base + t * rb, rb), rb), :] = (
                x * inv * gain + shift).astype(BF16)
            return carry

        lax.fori_loop(0, tm // rb, norm_rows, 0)

        pba = _dot_nt(hn_ref[tile, :], wba_ref[0])
        beta = _sigmoid(pba)
        z = pba + dtb_ref[...]
        softplus = jnp.maximum(z, 0.0) + jnp.log(1.0 + jnp.exp(-jnp.abs(z)))
        g = -jnp.exp(alog_ref[...]) * softplus
        pos = lax.broadcasted_iota(jnp.int32, (tm, LANES), 0) & (CHUNK - 1)
        pre = g
        suf = g
        s = 1
        while s < CHUNK:
            pre = pre + jnp.where(pos >= s, pltpu.roll(pre, s, 0), 0.0)
            suf = suf + jnp.where(pos < CHUNK - s, pltpu.roll(suf, tm - s, 0), 0.0)
            s *= 2
        lane = lax.broadcasted_iota(jnp.int32, (tm, LANES), 1)
        ba_ref[...] = jnp.where(lane < 2 * N_HEADS, beta,
                                jnp.where(lane < 3 * N_HEADS, pre,
                                          jnp.where(lane < 4 * N_HEADS, suf, 0.0)))

    @pl.when(i == 0)
    def _():
        wbf_ref[...] = w_ref[0].astype(BF16)

    proj_ref[...] = _dot_nt(hn_ref[tile, :], wbf_ref[...]).astype(proj_ref.dtype)


def _in_proj_call(x, mod, norm_g, w_t, alog_row, dtb_row, layer, main_width, skip, tokens_per_mod):
    n, d = x.shape
    gate_start = main_width + skip
    gate_width = w_t.shape[1] - gate_start
    tm, tn = 1024, 512
    n_tok = n // tm
    n_main = main_width // tn
    n_gate = gate_width // tn
    tok_tile = lambda j, i: jnp.where(j == 0, i, n_tok - 1)
    assert gate_start % SUBLANES == 0
    w_row = lambda j: pl.multiple_of(
        jnp.where(j < n_main, j * tn, gate_start + (j - n_main) * tn), SUBLANES)
    return pl.pallas_call(
        functools.partial(_in_proj_kernel, tm=tm, d=d),
        grid=(n_main + n_gate, n_tok),
        in_specs=[pl.BlockSpec((tm, d), lambda j, i: (tok_tile(j, i), 0)),
                  pl.BlockSpec((None, 1, 3 * d),
                               lambda j, i: (_mod_row(tok_tile(j, i), tm, tokens_per_mod), 0, 0)),
                  pl.BlockSpec((1, d), lambda j, i: (0, 0)),
                  pl.BlockSpec((pl.Element(1), pl.Element(tn), pl.Element(d)),
                               lambda j, i: (layer, w_row(j), 0)),
                  pl.BlockSpec((pl.Element(1), pl.Element(LANES), pl.Element(d)),
                               lambda j, i: (layer, main_width, 0)),
                  pl.BlockSpec((1, LANES), lambda j, i: (0, 0)),
                  pl.BlockSpec((1, LANES), lambda j, i: (0, 0))],
        out_specs=[pl.BlockSpec((tm, tn),
                                lambda j, i: (i, jnp.where(j < n_main, j + n_gate, j - n_main))),
                   pl.BlockSpec((tm, LANES), lambda j, i: (tok_tile(j, i), 0))],
        out_shape=[jax.ShapeDtypeStruct((n, gate_width + main_width), BF16),
                   jax.ShapeDtypeStruct((n, LANES), F32)],
        scratch_shapes=[pltpu.VMEM((n, d), BF16), pltpu.VMEM((tn, d), BF16)],
        compiler_params=_params(("arbitrary", "arbitrary")),
        name="in_proj",
    )(x, mod, norm_g, w_t, w_t, alog_row, dtb_row)


def _ln_swish_gate(conv_ref, az_ref, g_ref, b_ref, o_ref, n_rows):
    rb = 16
    blocks = 4
    gain = g_ref[...]
    bias = b_ref[...]

    def rows_body(t, carry):
        rows = [pl.ds(pl.multiple_of((t * blocks + u) * rb, rb), rb) for u in range(blocks)]
        zs = [conv_ref[r, :] for r in rows]
        mus = [jnp.mean(z, axis=-1, keepdims=True) for z in zs]
        zcs = [z - mu for z, mu in zip(zs, mus)]
        variances = [jnp.mean(zc * zc, axis=-1, keepdims=True) for zc in zcs]
        for r, zc, var in zip(rows, zcs, variances):
            y = _silu(zc * lax.rsqrt(var + EPS) * gain + bias)
            o_ref[r, :] = (y * _silu(az_ref[r, :].astype(F32))).astype(o_ref.dtype)
        return carry

    lax.fori_loop(0, n_rows // (rb * blocks), rows_body, 0)


def _conv_seq_kernel(av_ref, ag_ref, az_ref, w_ref, cb_ref, g_ref, b_ref, o_ref,
                     pad_ref, shift_ref, conv_ref, *, seq, taps):
    half = taps // 2
    lead = CONV_LEAD
    ch = av_ref.shape[1]
    pad_ref[0:lead, :] = jnp.zeros((lead, ch), F32)
    pad_ref[lead + seq:lead + seq + lead, :] = jnp.zeros((lead, ch), F32)
    pad_ref[lead:lead + seq, :] = av_ref[...].astype(F32) * _sigmoid(ag_ref[...].astype(F32))
    rb = 128
    shifted_rows = seq + 2 * lead - SUBLANES

    def lane_block(c, carry):
        lanes = pl.ds(pl.multiple_of(c * LANES, LANES), LANES)
        for s in range(1, SUBLANES):
            shift_ref[s - 1, 0:shifted_rows, :] = pad_ref[pl.ds(s, shifted_rows), lanes]
        for r in range(seq // rb):
            acc = jnp.zeros((rb, LANES), F32)
            for j in range(taps):
                s = (lead - half + j) % SUBLANES
                base = r * rb + lead - half + j - s
                if s == 0:
                    src = pad_ref[pl.ds(base, rb), lanes]
                else:
                    src = shift_ref[s - 1, pl.ds(base, rb), :]
                acc = acc + w_ref[j:j + 1, lanes] * src
            conv_ref[r * rb:(r + 1) * rb, lanes] = acc + cb_ref[:, lanes]
        return carry

    lax.fori_loop(0, ch // LANES, lane_block, 0)
    _ln_swish_gate(conv_ref, az_ref, g_ref, b_ref, o_ref, seq)


def _conv_axial_kernel(av_ref, ag_ref, az_ref, w_ref, cb_ref, g_ref, b_ref, o_ref,
                       pad_ref, shift_ref, ver_ref, conv_ref, *, grid_h, taps):
    half = taps // 2
    lead = CONV_LEAD
    shifted_rows = GRID_W + 2 * lead - SUBLANES
    ch = av_ref.shape[1]
    hc = ch // 2
    n_tok = grid_h * GRID_W
    for r in range(grid_h):
        rows = slice(r * GRID_W, (r + 1) * GRID_W)
        ua = av_ref[rows, :].astype(F32) * _sigmoid(ag_ref[rows, :].astype(F32))
        pad_ref[r, 0:lead, :] = jnp.zeros((lead, hc), F32)
        pad_ref[r, lead + GRID_W:lead + GRID_W + lead, :] = jnp.zeros((lead, hc), F32)
        pad_ref[r, lead:lead + GRID_W, :] = ua[:, 0:hc]
        ver_ref[rows, :] = ua[:, hc:ch]

    def lane_block(c, carry):
        lanes = pl.ds(pl.multiple_of(c * LANES, LANES), LANES)
        lanes_v = pl.ds(pl.multiple_of(hc + c * LANES, LANES), LANES)

        def grid_row(r, carry2):
            out_rows = pl.ds(pl.multiple_of(r * GRID_W, GRID_W), GRID_W)
            for s in range(1, SUBLANES):
                shift_ref[s - 1] = pad_ref[r, pl.ds(s, shifted_rows), lanes]
            acc = jnp.zeros((GRID_W, LANES), F32)
            for j in range(taps):
                s = (lead - half + j) % SUBLANES
                base = lead - half + j - s
                if s == 0:
                    src = pad_ref[r, pl.ds(base, GRID_W), lanes]
                else:
                    src = shift_ref[s - 1, pl.ds(base, GRID_W), :]
                acc = acc + w_ref[j:j + 1, lanes] * src
            conv_ref[out_rows, lanes] = acc + cb_ref[:, lanes]
            return carry2

        lax.fori_loop(0, grid_h, grid_row, 0)
        for r in range(grid_h):
            acc = jnp.zeros((GRID_W, LANES), F32)
            for src in range(grid_h):
                tap = src + half - r
                acc = acc + (w_ref[tap:tap + 1, lanes_v]
                             * ver_ref[src * GRID_W:(src + 1) * GRID_W, lanes])
            conv_ref[r * GRID_W:(r + 1) * GRID_W, lanes_v] = acc + cb_ref[:, lanes_v]
        return carry

    lax.fori_loop(0, hc // LANES, lane_block, 0)
    _ln_swish_gate(conv_ref, az_ref, g_ref, b_ref, o_ref, n_tok)


def _conv_call(proj, conv_w, conv_b, ln_g, ln_b, batch, seq, col0, axial):
    taps, ch = conv_w.shape
    proj3 = proj.reshape(batch, seq, proj.shape[1])
    row = lambda v: v.reshape(1, ch)
    if axial:
        grid_h = seq // GRID_W
        assert taps // 2 >= grid_h - 1, "every grid row must lie inside the conv window"
        kern = functools.partial(_conv_axial_kernel, grid_h=grid_h, taps=taps)
        scratch = [pltpu.VMEM((grid_h, GRID_W + 2 * CONV_LEAD, ch // 2), F32),
                   pltpu.VMEM((SUBLANES - 1, GRID_W + 2 * CONV_LEAD - SUBLANES, LANES), F32),
                   pltpu.VMEM((seq, ch // 2), F32),
                   pltpu.VMEM((seq, ch), F32)]
    else:
        kern = functools.partial(_conv_seq_kernel, seq=seq, taps=taps)
        scratch = [pltpu.VMEM((seq + 2 * CONV_LEAD, ch), F32),
                   pltpu.VMEM((SUBLANES - 1, seq + 2 * CONV_LEAD - SUBLANES, LANES), F32),
                   pltpu.VMEM((seq, ch), F32)]
    assert taps // 2 < CONV_LEAD
    tok = lambda k: pl.BlockSpec((None, seq, ch), lambda b: (b, 0, col0 + k))
    const = lambda shape: pl.BlockSpec(shape, lambda b: (0, 0))
    out = pl.pallas_call(
        kern,
        grid=(batch,),
        in_specs=[tok(0), tok(1), tok(2), const((taps, ch)), const((1, ch)), const((1, ch)),
                  const((1, ch))],
        out_specs=pl.BlockSpec((None, seq, ch), lambda b: (b, 0, 0)),
        out_shape=jax.ShapeDtypeStruct((batch, seq, ch), BF16),
        scratch_shapes=scratch,
        compiler_params=_params(("parallel",)),
        name="conv_axial" if axial else "conv_seq",
    )(proj3, proj3, proj3, conv_w, row(conv_b), row(ln_g), row(ln_b))
    return out.reshape(batch * seq, ch)


def _split_bf16(x):
    hi = x.astype(BF16)
    return hi, (x - hi.astype(F32)).astype(BF16)


def _dot_split(a_parts, b_parts):
    (a_hi, a_lo), (b_hi, b_lo) = a_parts, b_parts
    n = b_hi.shape[1]
    rhs = jnp.concatenate([jnp.concatenate([b_hi, b_lo], axis=1),
                           jnp.concatenate([b_hi, jnp.zeros_like(b_lo)], axis=1)], axis=0)
    both = jnp.dot(jnp.concatenate([a_hi, a_lo], axis=1), rhs, preferred_element_type=F32)
    return both[:, 0:n] + both[:, n:2 * n]


def _block_diag(parts, keep_left, keep_right):
    return tuple(jnp.concatenate([x * keep_left, x * keep_right], axis=0) for x in parts)


def _unit_triangular_inverse_pairs(n2s, eye2, keep_left, keep_right):
    ps = [eye2 - n2 for n2 in n2s]
    n_parts = [_split_bf16(n2) for n2 in n2s]
    ss = [_dot_split(parts, _block_diag(parts, keep_left, keep_right)) for parts in n_parts]
    power = 2
    while 2 * power < CHUNK:
        s_parts = [_split_bf16(s) for s in ss]
        p_parts = [_split_bf16(p) for p in ps]
        prods = [_dot_split(tuple(jnp.concatenate([pp, sp], axis=0) for pp, sp in zip(pp2, sp2)),
                            _block_diag(sp2, keep_left, keep_right))
                 for pp2, sp2 in zip(p_parts, s_parts)]
        ps = [p + pr[0:CHUNK] for p, pr in zip(ps, prods)]
        ss = [pr[CHUNK:2 * CHUNK] for pr in prods]
        power *= 2
    last = [_dot_split(_split_bf16(p), _block_diag(_split_bf16(s), keep_left, keep_right))
            for p, s in zip(ps, ss)]
    return [p + pr for p, pr in zip(ps, last)]


def _delta_kernel(*refs, seq, hb_count, chunk_unroll, zero_init):
    if zero_init:
        (q_ref, k_ref, v_ref, z_ref, ba_ref, bat_ref, wq_ref, wk_ref, wv_ref, hng_ref,
         y_ref, sout_ref, qs, ks, vs, gates, u_scr, wq_scr, lx_scr, egl_scr, o_scr, sscr) = refs
        s0_ref = None
    else:
        (q_ref, k_ref, v_ref, z_ref, ba_ref, bat_ref, wq_ref, wk_ref, wv_ref, hng_ref, s0_ref,
         y_ref, sout_ref, qs, ks, vs, gates, u_scr, wq_scr, lx_scr, egl_scr, o_scr, sscr) = refs
    nc = seq // CHUNK
    wq_rows = 2 * CHUNK
    lx_rows = CHUNK + HEAD_DIM
    h0 = pl.program_id(1) * hb_count

    row = lax.broadcasted_iota(jnp.int32, (seq, HEAD_DIM), 0)
    lane = lax.broadcasted_iota(jnp.int32, (seq, LANES), 1)

    def conv_swish(x_ref, w_ref, cols):
        x = x_ref[:, cols].astype(F32)
        prev = jnp.where(row >= 1, pltpu.roll(x, 1, 0), 0.0)
        nxt = jnp.where(row <= seq - 2, pltpu.roll(x, seq - 1, 0), 0.0)
        return _silu(w_ref[0:1, cols] * prev + w_ref[1:2, cols] * x + w_ref[2:3, cols] * nxt)

    def l2norm(x):
        return x * lax.rsqrt(jnp.sum(x * x, axis=-1, keepdims=True) + EPS)

    ba = ba_ref[...]
    for hb in range(hb_count):
        cols = slice(hb * HEAD_DIM, (hb + 1) * HEAD_DIM)
        qs[hb] = l2norm(conv_swish(q_ref, wq_ref, cols)) * (HEAD_DIM ** -0.5)
        ks[hb] = l2norm(conv_swish(k_ref, wk_ref, cols))
        vs[hb] = conv_swish(v_ref, wv_ref, cols)
        for t in range(4):
            col = jnp.sum(jnp.where(lane == t * N_HEADS + h0 + hb, ba, 0.0), axis=-1, keepdims=True)
            gates[t, hb] = jnp.broadcast_to(col, (seq, LANES))
        for d in range(2):
            if zero_init:
                sscr[d, hb] = jnp.zeros((HEAD_DIM, HEAD_DIM), F32)
            else:
                sscr[d, hb] = s0_ref[d, hb]

    ii = lax.broadcasted_iota(jnp.int32, (CHUNK, LANES), 0)
    l64 = lax.broadcasted_iota(jnp.int32, (CHUNK, LANES), 1)
    left = l64 < CHUNK
    right = l64 >= CHUNK
    jj = l64 & (CHUNK - 1)
    below = jnp.where(left, ii - jj, jj - ii)
    incl = below >= 0
    strict = below > 0
    eye2 = jnp.where(ii == jj, 1.0, 0.0).astype(F32)
    keep_left = jnp.where(left, 1.0, 0.0).astype(BF16)
    keep_right = jnp.where(left, 0.0, 1.0).astype(BF16)
    lane_t = lax.broadcasted_iota(jnp.int32, (HEAD_DIM, LANES), 1)
    left_t = lane_t < CHUNK
    right_t = lane_t >= CHUNK
    zeros = jnp.zeros((CHUNK, HEAD_DIM), F32)

    def chunk_products(problems):
        kqs = [_dot_nt(jnp.concatenate([k, q], axis=0), jnp.concatenate([k, k], axis=0))
               for q, k, *_ in problems]
        decays = []
        n2s = []
        for (q, k, v, bf, bb, gf, gb, grow2), kq in zip(problems, kqs):
            diff = jnp.where(left, gf, gb) - grow2
            decay = jnp.where(incl, jnp.exp(jnp.where(incl, diff, 0.0)), 0.0)
            decays.append(decay)
            n2s.append(jnp.where(strict, jnp.where(left, bf, bb) * kq[0:CHUNK] * decay, 0.0))
        tinvs = _unit_triangular_inverse_pairs(n2s, eye2, keep_left, keep_right)
        rhss = []
        for q, k, v, bf, bb, gf, gb, grow2 in problems:
            rhss.append(jnp.concatenate(
                [jnp.concatenate([v * bf, k * bf * jnp.exp(gf), zeros, zeros], axis=1),
                 jnp.concatenate([zeros, zeros, v * bb, k * bb * jnp.exp(gb)], axis=1)], axis=0))
        uws = [_dot(tinv, rhs) for tinv, rhs in zip(tinvs, rhss)]
        results = []
        for (q, k, v, bf, bb, gf, gb, grow2), kq, decay, uw in zip(problems, kqs, decays, uws):
            glf = gf[CHUNK - 1:CHUNK, :]
            glb = gb[0:1, :]
            kt2 = jnp.concatenate([k * jnp.exp(glf - gf), k * jnp.exp(glb - gb)], axis=0).T
            qkm = kq[CHUNK:2 * CHUNK] * decay
            out = []
            for d, (g, gl) in enumerate(((gf, glf), (gb, glb))):
                keep = left if d == 0 else right
                keep_t = left_t if d == 0 else right_t
                u = uw[:, (2 * d) * HEAD_DIM:(2 * d + 1) * HEAD_DIM]
                w = uw[:, (2 * d + 1) * HEAD_DIM:(2 * d + 2) * HEAD_DIM]
                wq = jnp.concatenate([w, q * jnp.exp(g)], axis=0).astype(BF16)
                lx = jnp.concatenate([jnp.where(keep, qkm, 0.0), jnp.where(keep_t, kt2, 0.0)],
                                     axis=0).astype(BF16)
                out.append((u, wq, lx, jnp.broadcast_to(jnp.exp(gl), (8, LANES))))
            results.append(out)
        return results

    def phase1(i, carry):
        where = []
        problems = []
        for j in range(chunk_unroll):
            c = i * chunk_unroll + j
            rows = pl.ds(pl.multiple_of(c * CHUNK, CHUNK), CHUNK)
            for hb in range(hb_count):
                where.append((c, rows, hb))
                problems.append((qs[hb, rows, :], ks[hb, rows, :], vs[hb, rows, :],
                                 gates[0, hb, rows, :], gates[1, hb, rows, :],
                                 gates[2, hb, rows, :], gates[3, hb, rows, :],
                                 bat_ref[h0 + hb, c]))
        for (c, rows, hb), res in zip(where, chunk_products(problems)):
            for d, (u, wq, lx, egl) in enumerate(res):
                u_scr[d, hb, rows, :] = u
                wq_scr[d, hb, pl.ds(pl.multiple_of(c * wq_rows, wq_rows), wq_rows), :] = wq
                lx_scr[d, hb, pl.ds(pl.multiple_of(c * lx_rows, CHUNK), lx_rows), :] = lx
                egl_scr[d, hb, c] = egl
        return carry

    lax.fori_loop(0, nc // chunk_unroll, phase1, 0)

    def phase2(c, carry):
        work = []
        for hb in range(hb_count):
            for d in range(2):
                cd = c if d == 0 else nc - 1 - c
                rows = pl.ds(pl.multiple_of(cd * CHUNK, CHUNK), CHUNK)
                work.append((hb, d, rows, sscr[d, hb], u_scr[d, hb, rows, :],
                             wq_scr[d, hb, pl.ds(pl.multiple_of(cd * wq_rows, wq_rows), wq_rows), :],
                             lx_scr[d, hb, pl.ds(pl.multiple_of(cd * lx_rows, CHUNK), lx_rows), :],
                             egl_scr[d, hb, cd]))
        wss = [jnp.dot(wq, state.astype(BF16), preferred_element_type=F32)
               for _, _, _, state, _, wq, _, _ in work]
        v_news = [(u - ws[0:CHUNK]).astype(BF16)
                  for (_, _, _, _, u, _, _, _), ws in zip(work, wss)]
        oms = [jnp.dot(lx, jnp.concatenate([v_new, v_new], axis=0), preferred_element_type=F32)
               for (_, _, _, _, _, _, lx, _), v_new in zip(work, v_news)]
        for (hb, d, rows, state, _, _, _, egl), ws, om in zip(work, wss, oms):
            o_scr[d, hb, rows, :] = ws[CHUNK:2 * CHUNK] + om[0:CHUNK]
            sscr[d, hb] = state * egl[0:1, :] + om[CHUNK:lx_rows]
        return carry

    lax.fori_loop(0, nc, phase2, 0)

    for hb in range(hb_count):
        cols = slice(hb * HEAD_DIM, (hb + 1) * HEAD_DIM)
        o = o_scr[0, hb] + o_scr[1, hb]
        o = o * lax.rsqrt(jnp.mean(o * o, axis=-1, keepdims=True) + EPS) * hng_ref[...]
        y_ref[:, cols] = (o * _silu(z_ref[:, cols].astype(F32))).astype(y_ref.dtype)
        for d in range(2):
            sout_ref[d, hb] = sscr[d, hb]


def _delta_call(proj, ba, conv_qkv_w, head_norm_g, s0, batch, seq, col_q):
    nc = seq // CHUNK
    hb_count = N_HEADS if seq <= 256 else N_HEADS // 2
    chunk_unroll = PHASE1_PROBLEMS // hb_count
    n_hblk = N_HEADS // hb_count
    bw = hb_count * HEAD_DIM
    kd = N_HEADS * HEAD_DIM
    proj3 = proj.reshape(batch, seq, proj.shape[1])
    ba3 = ba.reshape(batch, seq, LANES)
    g_rows = lambda lo: ba[:, lo:lo + N_HEADS].T.reshape(N_HEADS, batch, nc, 1, CHUNK)
    bat = jnp.concatenate([g_rows(2 * N_HEADS), g_rows(3 * N_HEADS)], axis=-1)
    cq = col_q // hb_count
    tok = lambda k: pl.BlockSpec((None, seq, bw), lambda b, h: (b, 0, cq + k * n_hblk + h))
    cw = lambda k: pl.BlockSpec((3, bw), lambda b, h: (0, k * n_hblk + h))
    in_specs = [tok(0), tok(1), tok(2), tok(3),
                pl.BlockSpec((None, seq, LANES), lambda b, h: (b, 0, 0)),
                pl.BlockSpec((N_HEADS, None, nc, 1, LANES), lambda b, h: (0, b, 0, 0, 0)),
                cw(0), cw(1), cw(2),
                pl.BlockSpec((1, HEAD_DIM), lambda b, h: (0, 0))]
    args = [proj3, proj3, proj3, proj3, ba3, bat, conv_qkv_w, conv_qkv_w, conv_qkv_w,
            head_norm_g.reshape(1, HEAD_DIM)]
    state_spec = pl.BlockSpec((None, 2, hb_count, HEAD_DIM, HEAD_DIM), lambda b, h: (b, 0, h, 0, 0))
    if s0 is not None:
        in_specs.append(state_spec)
        args.append(s0)
    head_scr = pltpu.VMEM((hb_count, seq, HEAD_DIM), F32)
    dir_scr = pltpu.VMEM((2, hb_count, seq, HEAD_DIM), F32)
    y, s_out = pl.pallas_call(
        functools.partial(_delta_kernel, seq=seq, hb_count=hb_count, chunk_unroll=chunk_unroll,
                          zero_init=s0 is None),
        grid=(batch, n_hblk),
        in_specs=in_specs,
        out_specs=[pl.BlockSpec((None, seq, bw), lambda b, h: (b, 0, h)), state_spec],
        out_shape=[jax.ShapeDtypeStruct((batch, seq, kd), BF16),
                   jax.ShapeDtypeStruct((batch, 2, N_HEADS, HEAD_DIM, HEAD_DIM), F32)],
        scratch_shapes=[head_scr, head_scr, head_scr,
                        pltpu.VMEM((4, hb_count, seq, LANES), F32),
                        dir_scr,
                        pltpu.VMEM((2, hb_count, nc * 2 * CHUNK, HEAD_DIM), BF16),
                        pltpu.VMEM((2, hb_count, nc * (CHUNK + HEAD_DIM), LANES), BF16),
                        pltpu.VMEM((2, hb_count, nc, 8, LANES), F32),
                        dir_scr,
                        pltpu.VMEM((2, hb_count, HEAD_DIM, HEAD_DIM), F32)],
        compiler_params=_params(("parallel", "parallel")),
        name="delta",
    )(*args)
    return y.reshape(batch * seq, kd), s_out


def _out_proj_kernel(x_ref, ya_ref, yb_ref, ga_ref, gb_ref, mod_ref, wpa_ref, wpb_ref, wo_ref,
                     fg_ref, o_ref, *, d, final):
    gate = mod_ref[:, 2 * d:3 * d]
    out_a = jnp.dot(ya_ref[...], wpa_ref[...], preferred_element_type=F32)
    out_b = jnp.dot(yb_ref[...], wpb_ref[...], preferred_element_type=F32)
    merged = (_sigmoid(ga_ref[...].astype(F32)) * out_a
              + _sigmoid(gb_ref[...].astype(F32)) * out_b)
    x = x_ref[...] + gate * jnp.dot(merged.astype(BF16), wo_ref[...], preferred_element_type=F32)
    if final:
        x = x * lax.rsqrt(jnp.mean(x * x, axis=-1, keepdims=True) + EPS) * fg_ref[...]
    o_ref[...] = x


def _out_proj_call(x, ya, yb, proj, mod, w_pa, w_pb, w_o, final_g, tokens_per_mod, final):
    n, d = x.shape
    cc = ya.shape[1]
    tm = 512
    kern = functools.partial(_out_proj_kernel, d=d, final=final)
    resident = lambda shape: pl.BlockSpec(shape, lambda i: (0, 0), pipeline_mode=pl.Buffered(1))
    return pl.pallas_call(
        kern,
        grid=(n // tm,),
        in_specs=[pl.BlockSpec((tm, d), lambda i: (i, 0)),
                  pl.BlockSpec((tm, cc), lambda i: (i, 0)),
                  pl.BlockSpec((tm, cc), lambda i: (i, 0)),
                  pl.BlockSpec((tm, d), lambda i: (i, 0)),
                  pl.BlockSpec((tm, d), lambda i: (i, 1)),
                  pl.BlockSpec((None, 1, 3 * d), lambda i: (_mod_row(i, tm, tokens_per_mod), 0, 0)),
                  resident(w_pa.shape), resident(w_pb.shape), resident(w_o.shape),
                  pl.BlockSpec((1, d), lambda i: (0, 0))],
        out_specs=pl.BlockSpec((tm, d), lambda i: (i, 0)),
        out_shape=jax.ShapeDtypeStruct((n, d), F32),
        compiler_params=_params(("parallel",)),
        name="out_proj",
    )(x, ya, yb, proj, proj, mod, w_pa, w_pb, w_o, final_g)


def kernel(x_prompt, x_sample, state_delta, c, c_ctx, w_mod, b_mod, norm_g, w_in, conv_a_w,
           conv_a_b, ln_a_g, ln_a_b, w_pa, conv_qkv_w, a_log, dt_bias, head_norm_g, w_pb, w_o,
           final_norm_g):
    batch, seq, d = x_prompt.shape
    dec_batch, dec_seq, _ = x_sample.shape
    depth = w_mod.shape[0]
    cc = conv_a_w.shape[2]
    kd = N_HEADS * HEAD_DIM
    o_ba = 3 * cc + 4 * kd
    o_gate = o_ba + 4 * N_HEADS

    cv = jnp.concatenate([c_ctx[None, :], c, jnp.zeros((MOD_ROWS - 1 - dec_batch, d), F32)], axis=0)
    mod = _mod_call(cv, w_mod, b_mod).reshape(depth, MOD_ROWS, 1, 3 * d)

    w_t = jnp.swapaxes(w_in, 1, 2)
    gate_pad = ((0, 0), (2 * N_HEADS, LANES - 4 * N_HEADS))
    alog_rows = jnp.pad(a_log.reshape(depth, 2 * N_HEADS), gate_pad).reshape(depth, 1, LANES)
    dtb_rows = jnp.pad(dt_bias.reshape(depth, 2 * N_HEADS), gate_pad).reshape(depth, 1, LANES)
    w_pa16, w_pb16, w_o16 = w_pa.astype(BF16), w_pb.astype(BF16), w_o.astype(BF16)
    col_conv = (2 * d) // cc
    col_q = (2 * d + 3 * cc) // HEAD_DIM
    final_g = final_norm_g.reshape(1, d)

    def layer(x, l, latent):
        b, s = (dec_batch, dec_seq) if latent else (batch, seq)
        tokens_per_mod = dec_seq if latent else None
        proj, ba = _in_proj_call(x, mod[l], norm_g[l].reshape(1, d), w_t, alog_rows[l],
                                 dtb_rows[l], l, o_ba, o_gate - o_ba, tokens_per_mod)
        ya = _conv_call(proj, conv_a_w[l], conv_a_b[l], ln_a_g[l], ln_a_b[l], b, s, col_conv,
                        axial=latent)
        s0 = state_delta[:, l] if latent else None
        yb, s_out = _delta_call(proj, ba, conv_qkv_w[l], head_norm_g[l], s0, b, s, col_q)
        x = _out_proj_call(x, ya, yb, proj, mod[l], w_pa16[l], w_pb16[l], w_o16[l], final_g,
                           tokens_per_mod, final=(l == depth - 1))
        return x, s_out

    h = x_prompt.reshape(batch * seq, d)
    hs = x_sample.reshape(dec_batch * dec_seq, d)
    ctx_states = []
    for l in range(depth):
        h, st = layer(h, l, latent=False)
        ctx_states.append(st)
        hs, _ = layer(hs, l, latent=True)
    y_prompt = h.reshape(batch, seq, d)
    y_sample = hs.reshape(dec_batch, dec_seq, d)
    state_delta_new = jnp.stack(ctx_states, axis=1)
    return (y_prompt, y_sample, state_delta_new)
```

```python
import functools

import jax
import jax.numpy as jnp
from jax import lax
from jax.experimental import pallas as pl
from jax.experimental.pallas import tpu as pltpu

F32 = jnp.float32
BF16 = jnp.bfloat16

EPS = 1e-6
GRID_W = 64
CHUNK = 64
N_HEADS = 8
HEAD_DIM = 128
LANES = 128
SUBLANES = 8
CONV_LEAD = 16
MOD_ROWS = 8
PHASE1_CHUNKS = 4
V7X_VMEM_LIMIT_BYTES = 56 * 1024 * 1024


def _sigmoid(x):
    return jax.nn.sigmoid(x)


def _silu(x):
    return x * jax.nn.sigmoid(x)


def _params(semantics):
    return pltpu.CompilerParams(dimension_semantics=semantics,
                                vmem_limit_bytes=V7X_VMEM_LIMIT_BYTES)


def _dot(a, b):
    return jnp.dot(a.astype(BF16), b.astype(BF16), preferred_element_type=F32)


def _dot_nt(a, b):
    return lax.dot_general(a.astype(BF16), b.astype(BF16), (((1,), (1,)), ((), ())),
                           preferred_element_type=F32)


def _mod_kernel(cv_ref, w_ref, b_ref, o_ref):
    a = _silu(cv_ref[...])
    o_ref[...] = _dot(a, w_ref[...]) + b_ref[...]


def _mod_call(cv, w_mod, b_mod):
    depth, d, d3 = w_mod.shape
    tn = 1024
    return pl.pallas_call(
        _mod_kernel,
        grid=(depth, d3 // tn),
        in_specs=[pl.BlockSpec((MOD_ROWS, d), lambda l, j: (0, 0)),
                  pl.BlockSpec((None, d, tn), lambda l, j: (l, 0, j)),
                  pl.BlockSpec((None, 1, tn), lambda l, j: (l, 0, j))],
        out_specs=pl.BlockSpec((None, MOD_ROWS, tn), lambda l, j: (l, 0, j)),
        out_shape=jax.ShapeDtypeStruct((depth, MOD_ROWS, d3), F32),
        compiler_params=_params(("parallel", "parallel")),
        name="mod",
    )(cv, w_mod, b_mod.reshape(depth, 1, d3))


def _mod_row(i, tm, tokens_per_mod):
    if tokens_per_mod is None:
        return 0
    return 1 + (i * tm) // tokens_per_mod


def _in_proj_kernel(x_ref, mod_ref, ng_ref, w_ref, wba_ref, alog_ref, dtb_ref,
                    proj_ref, ba_ref, hn_ref, wbf_ref, *, tm, d):
    j = pl.program_id(0)
    i = pl.program_id(1)
    base = i * tm
    tile = pl.ds(pl.multiple_of(base, tm), tm)

    @pl.when(j == 0)
    def _():
        shift = mod_ref[:, 0:d]
        scale = mod_ref[:, d:2 * d]
        gain = ng_ref[...] * (1.0 + scale)
        rb = 64

        def norm_rows(t, carry):
            x = x_ref[pl.ds(pl.multiple_of(t * rb, rb), rb), :]
            inv = lax.rsqrt(jnp.mean(x * x, axis=-1, keepdims=True) + EPS)
            hn_ref[pl.ds(pl.multiple_of(base + t * rb, rb), rb), :] = (
                x * inv * gain + shift).astype(BF16)
            return carry

        lax.fori_loop(0, tm // rb, norm_rows, 0)

        pba = _dot_nt(hn_ref[tile, :], wba_ref[0])
        beta = _sigmoid(pba)
        z = pba + dtb_ref[...]
        softplus = jnp.maximum(z, 0.0) + jnp.log(1.0 + jnp.exp(-jnp.abs(z)))
        g = -jnp.exp(alog_ref[...]) * softplus
        pos = lax.broadcasted_iota(jnp.int32, (tm, LANES), 0) & (CHUNK - 1)
        pre = g
        suf = g
        s = 1
        while s < CHUNK:
            pre = pre + jnp.where(pos >= s, pltpu.roll(pre, s, 0), 0.0)
            suf = suf + jnp.where(pos < CHUNK - s, pltpu.roll(suf, tm - s, 0), 0.0)
            s *= 2
        lane = lax.broadcasted_iota(jnp.int32, (tm, LANES), 1)
        ba_ref[...] = jnp.where(lane < 2 * N_HEADS, beta,
                                jnp.where(lane < 3 * N_HEADS, pre,
                                          jnp.where(lane < 4 * N_HEADS, suf, 0.0)))

    @pl.when(i == 0)
    def _():
        wbf_ref[...] = w_ref[0].astype(BF16)

    proj_ref[...] = _dot_nt(hn_ref[tile, :], wbf_ref[...]).astype(proj_ref.dtype)


def _in_proj_call(x, mod, norm_g, w_t, alog_row, dtb_row, layer, main_width, skip, tokens_per_mod):
    n, d = x.shape
    gate_start = main_width + skip
    gate_width = w_t.shape[1] - gate_start
    tm, tn = 1024, 512
    n_tok = n // tm
    n_main = main_width // tn
    n_gate = gate_width // tn
    tok_tile = lambda j, i: jnp.where(j == 0, i, n_tok - 1)
    assert gate_start % SUBLANES == 0
    w_row = lambda j: pl.multiple_of(
        jnp.where(j < n_main, j * tn, gate_start + (j - n_main) * tn), SUBLANES)
    return pl.pallas_call(
        functools.partial(_in_proj_kernel, tm=tm, d=d),
        grid=(n_main + n_gate, n_tok),
        in_specs=[pl.BlockSpec((tm, d), lambda j, i: (tok_tile(j, i), 0)),
                  pl.BlockSpec((None, 1, 3 * d),
                               lambda j, i: (_mod_row(tok_tile(j, i), tm, tokens_per_mod), 0, 0)),
                  pl.BlockSpec((1, d), lambda j, i: (0, 0)),
                  pl.BlockSpec((pl.Element(1), pl.Element(tn), pl.Element(d)),
                               lambda j, i: (layer, w_row(j), 0)),
                  pl.BlockSpec((pl.Element(1), pl.Element(LANES), pl.Element(d)),
                               lambda j, i: (layer, main_width, 0)),
                  pl.BlockSpec((1, LANES), lambda j, i: (0, 0)),
                  pl.BlockSpec((1, LANES), lambda j, i: (0, 0))],
        out_specs=[pl.BlockSpec((tm, tn),
                                lambda j, i: (i, jnp.where(j < n_main, j + n_gate, j - n_main))),
                   pl.BlockSpec((tm, LANES), lambda j, i: (tok_tile(j, i), 0))],
        out_shape=[jax.ShapeDtypeStruct((n, gate_width + main_width), BF16),
                   jax.ShapeDtypeStruct((n, LANES), F32)],
        scratch_shapes=[pltpu.VMEM((n, d), BF16), pltpu.VMEM((tn, d), BF16)],
        compiler_params=_params(("arbitrary", "arbitrary")),
        name="in_proj",
    )(x, mod, norm_g, w_t, w_t, alog_row, dtb_row)


def _ln_swish_gate(conv_ref, az_ref, g_ref, b_ref, o_ref, n_rows):
    rb = 16
    blocks = 8
    gain = g_ref[...]
    bias = b_ref[...]

    def rows_body(t, carry):
        rows = [pl.ds(pl.multiple_of((t * blocks + u) * rb, rb), rb) for u in range(blocks)]
        zs = [conv_ref[r, :] for r in rows]
        mus = [jnp.mean(z, axis=-1, keepdims=True) for z in zs]
        zcs = [z - mu for z, mu in zip(zs, mus)]
        variances = [jnp.mean(zc * zc, axis=-1, keepdims=True) for zc in zcs]
        for r, zc, var in zip(rows, zcs, variances):
            y = _silu(zc * lax.rsqrt(var + EPS) * gain + bias)
            o_ref[r, :] = (y * _silu(az_ref[r, :].astype(F32))).astype(o_ref.dtype)
        return carry

    lax.fori_loop(0, n_rows // (rb * blocks), rows_body, 0)


def _conv_seq_kernel(av_ref, ag_ref, az_ref, w_ref, cb_ref, g_ref, b_ref, o_ref,
                     pad_ref, shift_ref, conv_ref, *, seq, taps):
    half = taps // 2
    lead = CONV_LEAD
    ch = av_ref.shape[1]
    pad_ref[0:lead, :] = jnp.zeros((lead, ch), F32)
    pad_ref[lead + seq:lead + seq + lead, :] = jnp.zeros((lead, ch), F32)
    pad_ref[lead:lead + seq, :] = av_ref[...].astype(F32) * _sigmoid(ag_ref[...].astype(F32))
    rb = 128
    shifted_rows = seq + 2 * lead - SUBLANES

    def lane_block(c, carry):
        lanes = pl.ds(pl.multiple_of(c * LANES, LANES), LANES)
        for s in range(1, SUBLANES):
            shift_ref[s - 1, 0:shifted_rows, :] = pad_ref[pl.ds(s, shifted_rows), lanes]
        for r in range(seq // rb):
            acc = jnp.zeros((rb, LANES), F32)
            for j in range(taps):
                s = (lead - half + j) % SUBLANES
                base = r * rb + lead - half + j - s
                if s == 0:
                    src = pad_ref[pl.ds(base, rb), lanes]
                else:
                    src = shift_ref[s - 1, pl.ds(base, rb), :]
                acc = acc + w_ref[j:j + 1, lanes] * src
            conv_ref[r * rb:(r + 1) * rb, lanes] = acc + cb_ref[:, lanes]
        return carry

    lax.fori_loop(0, ch // LANES, lane_block, 0)
    _ln_swish_gate(conv_ref, az_ref, g_ref, b_ref, o_ref, seq)


def _conv_axial_kernel(av_ref, ag_ref, az_ref, w_ref, cb_ref, g_ref, b_ref, o_ref,
                       pad_ref, shift_ref, ver_ref, conv_ref, *, grid_h, taps):
    half = taps // 2
    lead = CONV_LEAD
    shifted_rows = GRID_W + 2 * lead - SUBLANES
    ch = av_ref.shape[1]
    hc = ch // 2
    n_tok = grid_h * GRID_W
    for r in range(grid_h):
        rows = slice(r * GRID_W, (r + 1) * GRID_W)
        ua = av_ref[rows, :].astype(F32) * _sigmoid(ag_ref[rows, :].astype(F32))
        pad_ref[r, 0:lead, :] = jnp.zeros((lead, hc), F32)
        pad_ref[r, lead + GRID_W:lead + GRID_W + lead, :] = jnp.zeros((lead, hc), F32)
        pad_ref[r, lead:lead + GRID_W, :] = ua[:, 0:hc]
        ver_ref[rows, :] = ua[:, hc:ch]

    def lane_block(c, carry):
        lanes = pl.ds(pl.multiple_of(c * LANES, LANES), LANES)
        lanes_v = pl.ds(pl.multiple_of(hc + c * LANES, LANES), LANES)

        def grid_row(r, carry2):
            out_rows = pl.ds(pl.multiple_of(r * GRID_W, GRID_W), GRID_W)
            for s in range(1, SUBLANES):
                shift_ref[s - 1] = pad_ref[r, pl.ds(s, shifted_rows), lanes]
            acc = jnp.zeros((GRID_W, LANES), F32)
            for j in range(taps):
                s = (lead - half + j) % SUBLANES
                base = lead - half + j - s
                if s == 0:
                    src = pad_ref[r, pl.ds(base, GRID_W), lanes]
                else:
                    src = shift_ref[s - 1, pl.ds(base, GRID_W), :]
                acc = acc + w_ref[j:j + 1, lanes] * src
            conv_ref[out_rows, lanes] = acc + cb_ref[:, lanes]
            return carry2

        lax.fori_loop(0, grid_h, grid_row, 0)
        for r in range(grid_h):
            acc = jnp.zeros((GRID_W, LANES), F32)
            for src in range(grid_h):
                tap = src + half - r
                acc = acc + (w_ref[tap:tap + 1, lanes_v]
                             * ver_ref[src * GRID_W:(src + 1) * GRID_W, lanes])
            conv_ref[r * GRID_W:(r + 1) * GRID_W, lanes_v] = acc + cb_ref[:, lanes_v]
        return carry

    lax.fori_loop(0, hc // LANES, lane_block, 0)
    _ln_swish_gate(conv_ref, az_ref, g_ref, b_ref, o_ref, n_tok)


def _conv_call(proj, conv_w, conv_b, ln_g, ln_b, batch, seq, col0, axial):
    taps, ch = conv_w.shape
    proj3 = proj.reshape(batch, seq, proj.shape[1])
    row = lambda v: v.reshape(1, ch)
    if axial:
        grid_h = seq // GRID_W
        assert taps // 2 >= grid_h - 1, "every grid row must lie inside the conv window"
        kern = functools.partial(_conv_axial_kernel, grid_h=grid_h, taps=taps)
        scratch = [pltpu.VMEM((grid_h, GRID_W + 2 * CONV_LEAD, ch // 2), F32),
                   pltpu.VMEM((SUBLANES - 1, GRID_W + 2 * CONV_LEAD - SUBLANES, LANES), F32),
                   pltpu.VMEM((seq, ch // 2), F32),
                   pltpu.VMEM((seq, ch), F32)]
    else:
        kern = functools.partial(_conv_seq_kernel, seq=seq, taps=taps)
        scratch = [pltpu.VMEM((seq + 2 * CONV_LEAD, ch), F32),
                   pltpu.VMEM((SUBLANES - 1, seq + 2 * CONV_LEAD - SUBLANES, LANES), F32),
                   pltpu.VMEM((seq, ch), F32)]
    assert taps // 2 < CONV_LEAD
    tok = lambda k: pl.BlockSpec((None, seq, ch), lambda b: (b, 0, col0 + k))
    const = lambda shape: pl.BlockSpec(shape, lambda b: (0, 0))
    out = pl.pallas_call(
        kern,
        grid=(batch,),
        in_specs=[tok(0), tok(1), tok(2), const((taps, ch)), const((1, ch)), const((1, ch)),
                  const((1, ch))],
        out_specs=pl.BlockSpec((None, seq, ch), lambda b: (b, 0, 0)),
        out_shape=jax.ShapeDtypeStruct((batch, seq, ch), BF16),
        scratch_shapes=scratch,
        compiler_params=_params(("parallel",)),
        name="conv_axial" if axial else "conv_seq",
    )(proj3, proj3, proj3, conv_w, row(conv_b), row(ln_g), row(ln_b))
    return out.reshape(batch * seq, ch)


def _split_bf16(x):
    hi = x.astype(BF16)
    return hi, (x - hi.astype(F32)).astype(BF16)


def _dot_split(a_parts, b_parts):
    (a_hi, a_lo), (b_hi, b_lo) = a_parts, b_parts
    n = b_hi.shape[1]
    rhs = jnp.concatenate([jnp.concatenate([b_hi, b_lo], axis=1),
                           jnp.concatenate([b_hi, jnp.zeros_like(b_lo)], axis=1)], axis=0)
    both = jnp.dot(jnp.concatenate([a_hi, a_lo], axis=1), rhs, preferred_element_type=F32)
    return both[:, 0:n] + both[:, n:2 * n]


def _block_diag(parts, keep_left, keep_right):
    return tuple(jnp.concatenate([x * keep_left, x * keep_right], axis=0) for x in parts)


def _unit_triangular_inverse_pairs(n2s, eye2, keep_left, keep_right):
    ps = [eye2 - n2 for n2 in n2s]
    n_parts = [_split_bf16(n2) for n2 in n2s]
    ss = [_dot_split(parts, _block_diag(parts, keep_left, keep_right)) for parts in n_parts]
    power = 2
    while 2 * power < CHUNK:
        s_parts = [_split_bf16(s) for s in ss]
        p_parts = [_split_bf16(p) for p in ps]
        prods = [_dot_split(tuple(jnp.concatenate([pp, sp], axis=0) for pp, sp in zip(pp2, sp2)),
                            _block_diag(sp2, keep_left, keep_right))
                 for pp2, sp2 in zip(p_parts, s_parts)]
        ps = [p + pr[0:CHUNK] for p, pr in zip(ps, prods)]
        ss = [pr[CHUNK:2 * CHUNK] for pr in prods]
        power *= 2
    last = [_dot_split(_split_bf16(p), _block_diag(_split_bf16(s), keep_left, keep_right))
            for p, s in zip(ps, ss)]
    return [p + pr for p, pr in zip(ps, last)]


def _delta_kernel(*refs, seq, hb_count, chunk_unroll, zero_init):
    if zero_init:
        (q_ref, k_ref, v_ref, z_ref, ba_ref, bat_ref, wq_ref, wk_ref, wv_ref, hng_ref,
         y_ref, sout_ref, qs, ks, vs, gates, u_scr, wq_scr, lx_scr, egl_scr, o_scr, sscr) = refs
        s0_ref = None
    else:
        (q_ref, k_ref, v_ref, z_ref, ba_ref, bat_ref, wq_ref, wk_ref, wv_ref, hng_ref, s0_ref,
         y_ref, sout_ref, qs, ks, vs, gates, u_scr, wq_scr, lx_scr, egl_scr, o_scr, sscr) = refs
    nc = seq // CHUNK
    wq_rows = 2 * CHUNK
    lx_rows = CHUNK + HEAD_DIM
    h0 = pl.program_id(1) * hb_count

    row8 = lax.broadcasted_iota(jnp.int32, (SUBLANES, HEAD_DIM), 0)
    lane = lax.broadcasted_iota(jnp.int32, (1, LANES), 1)

    def conv_swish(x_ref, w_ref, cols):
        x = x_ref[:, cols].astype(F32)
        prev = pltpu.roll(x, 1, 0)
        prev = jnp.concatenate([jnp.where(row8 >= 1, prev[0:SUBLANES], 0.0), prev[SUBLANES:]], axis=0)
        nxt = pltpu.roll(x, seq - 1, 0)
        nxt = jnp.concatenate([nxt[0:seq - SUBLANES],
                               jnp.where(row8 < SUBLANES - 1, nxt[seq - SUBLANES:], 0.0)], axis=0)
        return _silu(w_ref[0:1, cols] * prev + w_ref[1:2, cols] * x + w_ref[2:3, cols] * nxt)

    def l2norm(x):
        return x * lax.rsqrt(jnp.sum(x * x, axis=-1, keepdims=True) + EPS)

    ba = ba_ref[...]
    for hb in range(hb_count):
        cols = slice(hb * HEAD_DIM, (hb + 1) * HEAD_DIM)
        qs[hb] = l2norm(conv_swish(q_ref, wq_ref, cols)) * (HEAD_DIM ** -0.5)
        ks[hb] = l2norm(conv_swish(k_ref, wk_ref, cols))
        vs[hb] = conv_swish(v_ref, wv_ref, cols)
        for t in range(4):
            col = jnp.sum(jnp.where(lane == t * N_HEADS + h0 + hb, ba, 0.0), axis=-1, keepdims=True)
            gates[t, hb] = jnp.broadcast_to(col, (seq, LANES))
        for d in range(2):
            if zero_init:
                sscr[d, hb] = jnp.zeros((HEAD_DIM, HEAD_DIM), F32)
            else:
                sscr[d, hb] = s0_ref[d, hb]

    ii = lax.broadcasted_iota(jnp.int32, (CHUNK, LANES), 0)
    l64 = lax.broadcasted_iota(jnp.int32, (CHUNK, LANES), 1)
    left = l64 < CHUNK
    right = l64 >= CHUNK
    jj = l64 & (CHUNK - 1)
    below = jnp.where(left, ii - jj, jj - ii)
    incl = below >= 0
    strict = below > 0
    eye2 = jnp.where(ii == jj, 1.0, 0.0).astype(F32)
    keep_left = jnp.where(left, 1.0, 0.0).astype(BF16)
    keep_right = jnp.where(left, 0.0, 1.0).astype(BF16)
    lane_t = lax.broadcasted_iota(jnp.int32, (HEAD_DIM, LANES), 1)
    left_t = lane_t < CHUNK
    right_t = lane_t >= CHUNK
    zeros = jnp.zeros((CHUNK, HEAD_DIM), F32)

    def chunk_products(problems):
        kqs = [_dot_nt(jnp.concatenate([k, q], axis=0), jnp.concatenate([k, k], axis=0))
               for q, k, *_ in problems]
        decays = []
        n2s = []
        for (q, k, v, bf, bb, gf, gb, grow2), kq in zip(problems, kqs):
            diff = jnp.where(left, gf, gb) - grow2
            decay = jnp.where(incl, jnp.exp(jnp.where(incl, diff, 0.0)), 0.0)
            decays.append(decay)
            n2s.append(jnp.where(strict, jnp.where(left, bf, bb) * kq[0:CHUNK] * decay, 0.0))
        tinvs = _unit_triangular_inverse_pairs(n2s, eye2, keep_left, keep_right)
        rhss = []
        for q, k, v, bf, bb, gf, gb, grow2 in problems:
            rhss.append(jnp.concatenate(
                [jnp.concatenate([v * bf, k * bf * jnp.exp(gf), zeros, zeros], axis=1),
                 jnp.concatenate([zeros, zeros, v * bb, k * bb * jnp.exp(gb)], axis=1)], axis=0))
        uws = [_dot(tinv, rhs) for tinv, rhs in zip(tinvs, rhss)]
        results = []
        for (q, k, v, bf, bb, gf, gb, grow2), kq, decay, uw in zip(problems, kqs, decays, uws):
            glf = gf[CHUNK - 1:CHUNK, :]
            glb = gb[0:1, :]
            kt2 = jnp.concatenate([k * jnp.exp(glf - gf), k * jnp.exp(glb - gb)], axis=0).T
            qkm = kq[CHUNK:2 * CHUNK] * decay
            out = []
            for d, (g, gl) in enumerate(((gf, glf), (gb, glb))):
                keep = left if d == 0 else right
                keep_t = left_t if d == 0 else right_t
                u = uw[:, (2 * d) * HEAD_DIM:(2 * d + 1) * HEAD_DIM]
                w = uw[:, (2 * d + 1) * HEAD_DIM:(2 * d + 2) * HEAD_DIM]
                wq = jnp.concatenate([w, q * jnp.exp(g)], axis=0).astype(BF16)
                lx = jnp.concatenate([jnp.where(keep, qkm, 0.0), jnp.where(keep_t, kt2, 0.0)],
                                     axis=0).astype(BF16)
                out.append((u, wq, lx, jnp.broadcast_to(jnp.exp(gl), (8, LANES))))
            results.append(out)
        return results

    def phase1(i, carry):
        where = []
        problems = []
        for j in range(chunk_unroll):
            c = i * chunk_unroll + j
            rows = pl.ds(pl.multiple_of(c * CHUNK, CHUNK), CHUNK)
            for hb in range(hb_count):
                where.append((c, rows, hb))
                problems.append((qs[hb, rows, :], ks[hb, rows, :], vs[hb, rows, :],
                                 gates[0, hb, rows, :], gates[1, hb, rows, :],
                                 gates[2, hb, rows, :], gates[3, hb, rows, :],
                                 bat_ref[h0 + hb, c]))
        for (c, rows, hb), res in zip(where, chunk_products(problems)):
            for d, (u, wq, lx, egl) in enumerate(res):
                u_scr[d, hb, rows, :] = u
                wq_scr[d, hb, pl.ds(pl.multiple_of(c * wq_rows, wq_rows), wq_rows), :] = wq
                lx_scr[d, hb, pl.ds(pl.multiple_of(c * lx_rows, CHUNK), lx_rows), :] = lx
                egl_scr[d, hb, c] = egl
        return carry

    lax.fori_loop(0, nc // chunk_unroll, phase1, 0)

    def phase2(c, carry):
        work = []
        for hb in range(hb_count):
            for d in range(2):
                cd = c if d == 0 else nc - 1 - c
                rows = pl.ds(pl.multiple_of(cd * CHUNK, CHUNK), CHUNK)
                work.append((hb, d, rows, sscr[d, hb], u_scr[d, hb, rows, :],
                             wq_scr[d, hb, pl.ds(pl.multiple_of(cd * wq_rows, wq_rows), wq_rows), :],
                             lx_scr[d, hb, pl.ds(pl.multiple_of(cd * lx_rows, CHUNK), lx_rows), :],
                             egl_scr[d, hb, cd]))
        wss = [jnp.dot(wq, state.astype(BF16), preferred_element_type=F32)
               for _, _, _, state, _, wq, _, _ in work]
        v_news = [(u - ws[0:CHUNK]).astype(BF16)
                  for (_, _, _, _, u, _, _, _), ws in zip(work, wss)]
        oms = [jnp.dot(lx, jnp.concatenate([v_new, v_new], axis=0), preferred_element_type=F32)
               for (_, _, _, _, _, _, lx, _), v_new in zip(work, v_news)]
        for (hb, d, rows, state, _, _, _, egl), ws, om in zip(work, wss, oms):
            o_scr[d, hb, rows, :] = ws[CHUNK:2 * CHUNK] + om[0:CHUNK]
            sscr[d, hb] = state * egl[0:1, :] + om[CHUNK:lx_rows]
        return carry

    lax.fori_loop(0, nc, phase2, 0)

    for hb in range(hb_count):
        cols = slice(hb * HEAD_DIM, (hb + 1) * HEAD_DIM)
        o = o_scr[0, hb] + o_scr[1, hb]
        o = o * lax.rsqrt(jnp.mean(o * o, axis=-1, keepdims=True) + EPS) * hng_ref[...]
        y_ref[:, cols] = (o * _silu(z_ref[:, cols].astype(F32))).astype(y_ref.dtype)
        for d in range(2):
            sout_ref[d, hb] = sscr[d, hb]


def _delta_call(proj, ba, conv_qkv_w, head_norm_g, s0, batch, seq, col_q):
    nc = seq // CHUNK
    hb_count = N_HEADS if seq <= 256 else N_HEADS // 2
    chunk_unroll = PHASE1_CHUNKS
    n_hblk = N_HEADS // hb_count
    bw = hb_count * HEAD_DIM
    kd = N_HEADS * HEAD_DIM
    proj3 = proj.reshape(batch, seq, proj.shape[1])
    ba3 = ba.reshape(batch, seq, LANES)
    g_rows = lambda lo: ba[:, lo:lo + N_HEADS].T.reshape(N_HEADS, batch, nc, 1, CHUNK)
    bat = jnp.concatenate([g_rows(2 * N_HEADS), g_rows(3 * N_HEADS)], axis=-1)
    cq = col_q // hb_count
    tok = lambda k: pl.BlockSpec((None, seq, bw), lambda b, h: (b, 0, cq + k * n_hblk + h))
    cw = lambda k: pl.BlockSpec((3, bw), lambda b, h: (0, k * n_hblk + h))
    in_specs = [tok(0), tok(1), tok(2), tok(3),
                pl.BlockSpec((None, seq, LANES), lambda b, h: (b, 0, 0)),
                pl.BlockSpec((N_HEADS, None, nc, 1, LANES), lambda b, h: (0, b, 0, 0, 0)),
                cw(0), cw(1), cw(2),
                pl.BlockSpec((1, HEAD_DIM), lambda b, h: (0, 0))]
    args = [proj3, proj3, proj3, proj3, ba3, bat, conv_qkv_w, conv_qkv_w, conv_qkv_w,
            head_norm_g.reshape(1, HEAD_DIM)]
    state_spec = pl.BlockSpec((None, 2, hb_count, HEAD_DIM, HEAD_DIM), lambda b, h: (b, 0, h, 0, 0))
    if s0 is not None:
        in_specs.append(state_spec)
        args.append(s0)
    head_scr = pltpu.VMEM((hb_count, seq, HEAD_DIM), F32)
    dir_scr = pltpu.VMEM((2, hb_count, seq, HEAD_DIM), F32)
    y, s_out = pl.pallas_call(
        functools.partial(_delta_kernel, seq=seq, hb_count=hb_count, chunk_unroll=chunk_unroll,
                          zero_init=s0 is None),
        grid=(batch, n_hblk),
        in_specs=in_specs,
        out_specs=[pl.BlockSpec((None, seq, bw), lambda b, h: (b, 0, h)), state_spec],
        out_shape=[jax.ShapeDtypeStruct((batch, seq, kd), BF16),
                   jax.ShapeDtypeStruct((batch, 2, N_HEADS, HEAD_DIM, HEAD_DIM), F32)],
        scratch_shapes=[head_scr, head_scr, head_scr,
                        pltpu.VMEM((4, hb_count, seq, LANES), F32),
                        dir_scr,
                        pltpu.VMEM((2, hb_count, nc * 2 * CHUNK, HEAD_DIM), BF16),
                        pltpu.VMEM((2, hb_count, nc * (CHUNK + HEAD_DIM), LANES), BF16),
                        pltpu.VMEM((2, hb_count, nc, 8, LANES), F32),
                        dir_scr,
                        pltpu.VMEM((2, hb_count, HEAD_DIM, HEAD_DIM), F32)],
        compiler_params=_params(("parallel", "parallel")),
        name="delta",
    )(*args)
    return y.reshape(batch * seq, kd), s_out


def _out_proj_kernel(x_ref, ya_ref, yb_ref, ga_ref, gb_ref, mod_ref, wpa_ref, wpb_ref, wo_ref,
                     fg_ref, o_ref, *, d, final):
    gate = mod_ref[:, 2 * d:3 * d]
    out_a = jnp.dot(ya_ref[...], wpa_ref[...], preferred_element_type=F32)
    out_b = jnp.dot(yb_ref[...], wpb_ref[...], preferred_element_type=F32)
    merged = (_sigmoid(ga_ref[...].astype(F32)) * out_a
              + _sigmoid(gb_ref[...].astype(F32)) * out_b)
    x = x_ref[...] + gate * jnp.dot(merged.astype(BF16), wo_ref[...], preferred_element_type=F32)
    if final:
        x = x * lax.rsqrt(jnp.mean(x * x, axis=-1, keepdims=True) + EPS) * fg_ref[...]
    o_ref[...] = x


def _out_proj_call(x, ya, yb, proj, mod, w_pa, w_pb, w_o, final_g, tokens_per_mod, final):
    n, d = x.shape
    cc = ya.shape[1]
    tm = 512
    kern = functools.partial(_out_proj_kernel, d=d, final=final)
    resident = lambda shape: pl.BlockSpec(shape, lambda i: (0, 0), pipeline_mode=pl.Buffered(1))
    return pl.pallas_call(
        kern,
        grid=(n // tm,),
        in_specs=[pl.BlockSpec((tm, d), lambda i: (i, 0)),
                  pl.BlockSpec((tm, cc), lambda i: (i, 0)),
                  pl.BlockSpec((tm, cc), lambda i: (i, 0)),
                  pl.BlockSpec((tm, d), lambda i: (i, 0)),
                  pl.BlockSpec((tm, d), lambda i: (i, 1)),
                  pl.BlockSpec((None, 1, 3 * d), lambda i: (_mod_row(i, tm, tokens_per_mod), 0, 0)),
                  resident(w_pa.shape), resident(w_pb.shape), resident(w_o.shape),
                  pl.BlockSpec((1, d), lambda i: (0, 0))],
        out_specs=pl.BlockSpec((tm, d), lambda i: (i, 0)),
        out_shape=jax.ShapeDtypeStruct((n, d), F32),
        compiler_params=_params(("parallel",)),
        name="out_proj",
    )(x, ya, yb, proj, proj, mod, w_pa, w_pb, w_o, final_g)


def kernel(x_prompt, x_sample, state_delta, c, c_ctx, w_mod, b_mod, norm_g, w_in, conv_a_w,
           conv_a_b, ln_a_g, ln_a_b, w_pa, conv_qkv_w, a_log, dt_bias, head_norm_g, w_pb, w_o,
           final_norm_g):
    batch, seq, d = x_prompt.shape
    dec_batch, dec_seq, _ = x_sample.shape
    depth = w_mod.shape[0]
    cc = conv_a_w.shape[2]
    kd = N_HEADS * HEAD_DIM
    o_ba = 3 * cc + 4 * kd
    o_gate = o_ba + 4 * N_HEADS

    cv = jnp.concatenate([c_ctx[None, :], c, jnp.zeros((MOD_ROWS - 1 - dec_batch, d), F32)], axis=0)
    mod = _mod_call(cv, w_mod, b_mod).reshape(depth, MOD_ROWS, 1, 3 * d)

    w_t = jnp.swapaxes(w_in, 1, 2)
    gate_pad = ((0, 0), (2 * N_HEADS, LANES - 4 * N_HEADS))
    alog_rows = jnp.pad(a_log.reshape(depth, 2 * N_HEADS), gate_pad).reshape(depth, 1, LANES)
    dtb_rows = jnp.pad(dt_bias.reshape(depth, 2 * N_HEADS), gate_pad).reshape(depth, 1, LANES)
    w_pa16, w_pb16, w_o16 = w_pa.astype(BF16), w_pb.astype(BF16), w_o.astype(BF16)
    col_conv = (2 * d) // cc
    col_q = (2 * d + 3 * cc) // HEAD_DIM
    final_g = final_norm_g.reshape(1, d)

    def layer(x, l, latent):
        b, s = (dec_batch, dec_seq) if latent else (batch, seq)
        tokens_per_mod = dec_seq if latent else None
        proj, ba = _in_proj_call(x, mod[l], norm_g[l].reshape(1, d), w_t, alog_rows[l],
                                 dtb_rows[l], l, o_ba, o_gate - o_ba, tokens_per_mod)
        ya = _conv_call(proj, conv_a_w[l], conv_a_b[l], ln_a_g[l], ln_a_b[l], b, s, col_conv,
                        axial=latent)
        s0 = state_delta[:, l] if latent else None
        yb, s_out = _delta_call(proj, ba, conv_qkv_w[l], head_norm_g[l], s0, b, s, col_q)
        x = _out_proj_call(x, ya, yb, proj, mod[l], w_pa16[l], w_pb16[l], w_o16[l], final_g,
                           tokens_per_mod, final=(l == depth - 1))
        return x, s_out

    h = x_prompt.reshape(batch * seq, d)
    hs = x_sample.reshape(dec_batch * dec_seq, d)
    ctx_states = []
    for l in range(depth):
        h, st = layer(h, l, latent=False)
        ctx_states.append(st)
        hs, _ = layer(hs, l, latent=True)
    y_prompt = h.reshape(batch, seq, d)
    y_sample = hs.reshape(dec_batch, dec_seq, d)
    state_delta_new = jnp.stack(ctx_states, axis=1)
    return (y_prompt, y_sample, state_delta_new)
```

```python
import functools

import jax
import jax.numpy as jnp
from jax import lax
from jax.experimental import pallas as pl
from jax.experimental.pallas import tpu as pltpu

F32 = jnp.float32
BF16 = jnp.bfloat16

EPS = 1e-6
GRID_W = 64
CHUNK = 64
N_HEADS = 8
HEAD_DIM = 128
LANES = 128
SUBLANES = 8
CONV_LEAD = 16
MOD_ROWS = 8
PHASE1_CHUNKS = 4
V7X_VMEM_LIMIT_BYTES = 56 * 1024 * 1024


def _sigmoid(x):
    return jax.nn.sigmoid(x)


def _silu(x):
    return x * jax.nn.sigmoid(x)


def _params(semantics):
    return pltpu.CompilerParams(dimension_semantics=semantics,
                                vmem_limit_bytes=V7X_VMEM_LIMIT_BYTES)


def _dot(a, b):
    return jnp.dot(a.astype(BF16), b.astype(BF16), preferred_element_type=F32)


def _dot_nt(a, b):
    return lax.dot_general(a.astype(BF16), b.astype(BF16), (((1,), (1,)), ((), ())),
                           preferred_element_type=F32)


def _mod_kernel(cv_ref, w_ref, b_ref, o_ref):
    a = _silu(cv_ref[...])
    o_ref[...] = _dot(a, w_ref[...]) + b_ref[...]


def _mod_call(cv, w_mod, b_mod):
    depth, d, d3 = w_mod.shape
    tn = 1024
    return pl.pallas_call(
        _mod_kernel,
        grid=(depth, d3 // tn),
        in_specs=[pl.BlockSpec((MOD_ROWS, d), lambda l, j: (0, 0)),
                  pl.BlockSpec((None, d, tn), lambda l, j: (l, 0, j)),
                  pl.BlockSpec((None, 1, tn), lambda l, j: (l, 0, j))],
        out_specs=pl.BlockSpec((None, MOD_ROWS, tn), lambda l, j: (l, 0, j)),
        out_shape=jax.ShapeDtypeStruct((depth, MOD_ROWS, d3), F32),
        compiler_params=_params(("parallel", "parallel")),
        name="mod",
    )(cv, w_mod, b_mod.reshape(depth, 1, d3))


def _mod_row(i, tm, tokens_per_mod):
    if tokens_per_mod is None:
        return 0
    return 1 + (i * tm) // tokens_per_mod


def _in_proj_kernel(x_ref, mod_ref, ng_ref, w_ref, wba_ref, alog_ref, dtb_ref,
                    proj_ref, ba_ref, bat_ref, hn_ref, wbf_ref, *, tm, d):
    j = pl.program_id(0)
    i = pl.program_id(1)
    base = i * tm
    tile = pl.ds(pl.multiple_of(base, tm), tm)

    @pl.when(j == 0)
    def _():
        shift = mod_ref[:, 0:d]
        scale = mod_ref[:, d:2 * d]
        gain = ng_ref[...] * (1.0 + scale)
        rb = 64

        def norm_rows(t, carry):
            x = x_ref[pl.ds(pl.multiple_of(t * rb, rb), rb), :]
            inv = lax.rsqrt(jnp.mean(x * x, axis=-1, keepdims=True) + EPS)
            hn_ref[pl.ds(pl.multiple_of(base + t * rb, rb), rb), :] = (
                x * inv * gain + shift).astype(BF16)
            return carry

        lax.fori_loop(0, tm // rb, norm_rows, 0)

        pba = _dot_nt(hn_ref[tile, :], wba_ref[0])
        beta = _sigmoid(pba)
        z = pba + dtb_ref[...]
        softplus = jnp.maximum(z, 0.0) + jnp.log(1.0 + jnp.exp(-jnp.abs(z)))
        g = -jnp.exp(alog_ref[...]) * softplus
        pos = lax.broadcasted_iota(jnp.int32, (tm, LANES), 0) & (CHUNK - 1)
        pre = g
        suf = g
        s = 1
        while s < CHUNK:
            pre = pre + jnp.where(pos >= s, pltpu.roll(pre, s, 0), 0.0)
            suf = suf + jnp.where(pos < CHUNK - s, pltpu.roll(suf, tm - s, 0), 0.0)
            s *= 2
        lane = lax.broadcasted_iota(jnp.int32, (tm, LANES), 1)
        ba_ref[...] = jnp.where(lane < 2 * N_HEADS, beta,
                                jnp.where(lane < 3 * N_HEADS, pre,
                                          jnp.where(lane < 4 * N_HEADS, suf, 0.0)))
        fwd_half = lax.broadcasted_iota(jnp.int32, (N_HEADS, LANES), 1) < CHUNK
        for c in range(tm // CHUNK):
            rows = slice(c * CHUNK, (c + 1) * CHUNK)
            by_gate = jnp.concatenate([pre[rows], suf[rows]], axis=0).T
            both = jnp.where(fwd_half, by_gate[2 * N_HEADS:3 * N_HEADS],
                             by_gate[3 * N_HEADS:4 * N_HEADS])
            for h in range(N_HEADS):
                bat_ref[c, h] = both[h:h + 1, :]

    @pl.when(i == 0)
    def _():
        wbf_ref[...] = w_ref[0].astype(BF16)

    proj_ref[...] = _dot_nt(hn_ref[tile, :], wbf_ref[...]).astype(proj_ref.dtype)


def _in_proj_call(x, mod, norm_g, w_t, alog_row, dtb_row, layer, main_width, skip, tokens_per_mod):
    n, d = x.shape
    gate_start = main_width + skip
    gate_width = w_t.shape[1] - gate_start
    tm, tn = 1024, 512
    n_tok = n // tm
    n_main = main_width // tn
    n_gate = gate_width // tn
    tok_tile = lambda j, i: jnp.where(j == 0, i, n_tok - 1)
    assert gate_start % SUBLANES == 0
    w_row = lambda j: pl.multiple_of(
        jnp.where(j < n_main, j * tn, gate_start + (j - n_main) * tn), SUBLANES)
    return pl.pallas_call(
        functools.partial(_in_proj_kernel, tm=tm, d=d),
        grid=(n_main + n_gate, n_tok),
        in_specs=[pl.BlockSpec((tm, d), lambda j, i: (tok_tile(j, i), 0)),
                  pl.BlockSpec((None, 1, 3 * d),
                               lambda j, i: (_mod_row(tok_tile(j, i), tm, tokens_per_mod), 0, 0)),
                  pl.BlockSpec((1, d), lambda j, i: (0, 0)),
                  pl.BlockSpec((pl.Element(1), pl.Element(tn), pl.Element(d)),
                               lambda j, i: (layer, w_row(j), 0)),
                  pl.BlockSpec((pl.Element(1), pl.Element(LANES), pl.Element(d)),
                               lambda j, i: (layer, main_width, 0)),
                  pl.BlockSpec((1, LANES), lambda j, i: (0, 0)),
                  pl.BlockSpec((1, LANES), lambda j, i: (0, 0))],
        out_specs=[pl.BlockSpec((tm, tn),
                                lambda j, i: (i, jnp.where(j < n_main, j + n_gate, j - n_main))),
                   pl.BlockSpec((tm, LANES), lambda j, i: (tok_tile(j, i), 0)),
                   pl.BlockSpec((tm // CHUNK, N_HEADS, 1, LANES),
                                lambda j, i: (tok_tile(j, i), 0, 0, 0))],
        out_shape=[jax.ShapeDtypeStruct((n, gate_width + main_width), BF16),
                   jax.ShapeDtypeStruct((n, LANES), F32),
                   jax.ShapeDtypeStruct((n // CHUNK, N_HEADS, 1, LANES), F32)],
        scratch_shapes=[pltpu.VMEM((n, d), BF16), pltpu.VMEM((tn, d), BF16)],
        compiler_params=_params(("arbitrary", "arbitrary")),
        name="in_proj",
    )(x, mod, norm_g, w_t, w_t, alog_row, dtb_row)


def _ln_swish_gate(conv_ref, az_ref, g_ref, b_ref, o_ref, n_rows):
    rb = 16
    blocks = 8
    gain = g_ref[...]
    bias = b_ref[...]

    def rows_body(t, carry):
        rows = [pl.ds(pl.multiple_of((t * blocks + u) * rb, rb), rb) for u in range(blocks)]
        zs = [conv_ref[r, :] for r in rows]
        mus = [jnp.mean(z, axis=-1, keepdims=True) for z in zs]
        zcs = [z - mu for z, mu in zip(zs, mus)]
        variances = [jnp.mean(zc * zc, axis=-1, keepdims=True) for zc in zcs]
        for r, zc, var in zip(rows, zcs, variances):
            y = _silu(zc * lax.rsqrt(var + EPS) * gain + bias)
            o_ref[r, :] = (y * _silu(az_ref[r, :].astype(F32))).astype(o_ref.dtype)
        return carry

    lax.fori_loop(0, n_rows // (rb * blocks), rows_body, 0)


def _conv_seq_kernel(av_ref, ag_ref, az_ref, w_ref, cb_ref, g_ref, b_ref, o_ref,
                     pad_ref, shift_ref, conv_ref, *, seq, taps):
    half = taps // 2
    lead = CONV_LEAD
    ch = av_ref.shape[1]
    pad_ref[0:lead, :] = jnp.zeros((lead, ch), F32)
    pad_ref[lead + seq:lead + seq + lead, :] = jnp.zeros((lead, ch), F32)
    pad_ref[lead:lead + seq, :] = av_ref[...].astype(F32) * _sigmoid(ag_ref[...].astype(F32))
    rb = 128
    shifted_rows = seq + 2 * lead - SUBLANES

    def lane_block(c, carry):
        lanes = pl.ds(pl.multiple_of(c * LANES, LANES), LANES)
        for s in range(1, SUBLANES):
            shift_ref[s - 1, 0:shifted_rows, :] = pad_ref[pl.ds(s, shifted_rows), lanes]
        for r in range(seq // rb):
            acc = jnp.zeros((rb, LANES), F32)
            for j in range(taps):
                s = (lead - half + j) % SUBLANES
                base = r * rb + lead - half + j - s
                if s == 0:
                    src = pad_ref[pl.ds(base, rb), lanes]
                else:
                    src = shift_ref[s - 1, pl.ds(base, rb), :]
                acc = acc + w_ref[j:j + 1, lanes] * src
            conv_ref[r * rb:(r + 1) * rb, lanes] = acc + cb_ref[:, lanes]
        return carry

    lax.fori_loop(0, ch // LANES, lane_block, 0)
    _ln_swish_gate(conv_ref, az_ref, g_ref, b_ref, o_ref, seq)


def _conv_axial_kernel(av_ref, ag_ref, az_ref, w_ref, cb_ref, g_ref, b_ref, o_ref,
                       pad_ref, shift_ref, ver_ref, conv_ref, *, grid_h, taps):
    half = taps // 2
    lead = CONV_LEAD
    shifted_rows = GRID_W + 2 * lead - SUBLANES
    ch = av_ref.shape[1]
    hc = ch // 2
    n_tok = grid_h * GRID_W
    for r in range(grid_h):
        rows = slice(r * GRID_W, (r + 1) * GRID_W)
        ua = av_ref[rows, :].astype(F32) * _sigmoid(ag_ref[rows, :].astype(F32))
        pad_ref[r, 0:lead, :] = jnp.zeros((lead, hc), F32)
        pad_ref[r, lead + GRID_W:lead + GRID_W + lead, :] = jnp.zeros((lead, hc), F32)
        pad_ref[r, lead:lead + GRID_W, :] = ua[:, 0:hc]
        ver_ref[rows, :] = ua[:, hc:ch]

    def lane_block(c, carry):
        lanes = pl.ds(pl.multiple_of(c * LANES, LANES), LANES)
        lanes_v = pl.ds(pl.multiple_of(hc + c * LANES, LANES), LANES)

        def grid_row(r, carry2):
            out_rows = pl.ds(pl.multiple_of(r * GRID_W, GRID_W), GRID_W)
            for s in range(1, SUBLANES):
                shift_ref[s - 1] = pad_ref[r, pl.ds(s, shifted_rows), lanes]
            acc = jnp.zeros((GRID_W, LANES), F32)
            for j in range(taps):
                s = (lead - half + j) % SUBLANES
                base = lead - half + j - s
                if s == 0:
                    src = pad_ref[r, pl.ds(base, GRID_W), lanes]
                else:
                    src = shift_ref[s - 1, pl.ds(base, GRID_W), :]
                acc = acc + w_ref[j:j + 1, lanes] * src
            conv_ref[out_rows, lanes] = acc + cb_ref[:, lanes]
            return carry2

        lax.fori_loop(0, grid_h, grid_row, 0)
        for r in range(grid_h):
            acc = jnp.zeros((GRID_W, LANES), F32)
            for src in range(grid_h):
                tap = src + half - r
                acc = acc + (w_ref[tap:tap + 1, lanes_v]
                             * ver_ref[src * GRID_W:(src + 1) * GRID_W, lanes])
            conv_ref[r * GRID_W:(r + 1) * GRID_W, lanes_v] = acc + cb_ref[:, lanes_v]
        return carry

    lax.fori_loop(0, hc // LANES, lane_block, 0)
    _ln_swish_gate(conv_ref, az_ref, g_ref, b_ref, o_ref, n_tok)


def _conv_call(proj, conv_w, conv_b, ln_g, ln_b, batch, seq, col0, axial):
    taps, ch = conv_w.shape
    proj3 = proj.reshape(batch, seq, proj.shape[1])
    row = lambda v: v.reshape(1, ch)
    if axial:
        grid_h = seq // GRID_W
        assert taps // 2 >= grid_h - 1, "every grid row must lie inside the conv window"
        kern = functools.partial(_conv_axial_kernel, grid_h=grid_h, taps=taps)
        scratch = [pltpu.VMEM((grid_h, GRID_W + 2 * CONV_LEAD, ch // 2), F32),
                   pltpu.VMEM((SUBLANES - 1, GRID_W + 2 * CONV_LEAD - SUBLANES, LANES), F32),
                   pltpu.VMEM((seq, ch // 2), F32),
                   pltpu.VMEM((seq, ch), F32)]
    else:
        kern = functools.partial(_conv_seq_kernel, seq=seq, taps=taps)
        scratch = [pltpu.VMEM((seq + 2 * CONV_LEAD, ch), F32),
                   pltpu.VMEM((SUBLANES - 1, seq + 2 * CONV_LEAD - SUBLANES, LANES), F32),
                   pltpu.VMEM((seq, ch), F32)]
    assert taps // 2 < CONV_LEAD
    tok = lambda k: pl.BlockSpec((None, seq, ch), lambda b: (b, 0, col0 + k))
    const = lambda shape: pl.BlockSpec(shape, lambda b: (0, 0))
    out = pl.pallas_call(
        kern,
        grid=(batch,),
        in_specs=[tok(0), tok(1), tok(2), const((taps, ch)), const((1, ch)), const((1, ch)),
                  const((1, ch))],
        out_specs=pl.BlockSpec((None, seq, ch), lambda b: (b, 0, 0)),
        out_shape=jax.ShapeDtypeStruct((batch, seq, ch), BF16),
        scratch_shapes=scratch,
        compiler_params=_params(("parallel",)),
        name="conv_axial" if axial else "conv_seq",
    )(proj3, proj3, proj3, conv_w, row(conv_b), row(ln_g), row(ln_b))
    return out.reshape(batch * seq, ch)


def _split_bf16(x):
    hi = x.astype(BF16)
    return hi, (x - hi.astype(F32)).astype(BF16)


def _dot_split(a_parts, b_parts):
    (a_hi, a_lo), (b_hi, b_lo) = a_parts, b_parts
    n = b_hi.shape[1]
    rhs = jnp.concatenate([jnp.concatenate([b_hi, b_lo], axis=1),
                           jnp.concatenate([b_hi, jnp.zeros_like(b_lo)], axis=1)], axis=0)
    both = jnp.dot(jnp.concatenate([a_hi, a_lo], axis=1), rhs, preferred_element_type=F32)
    return both[:, 0:n] + both[:, n:2 * n]


def _block_diag(parts, keep_left, keep_right):
    return tuple(jnp.concatenate([x * keep_left, x * keep_right], axis=0) for x in parts)


def _unit_triangular_inverse_pairs(n2s, eye2, keep_left, keep_right):
    ps = [eye2 - n2 for n2 in n2s]
    n_parts = [_split_bf16(n2) for n2 in n2s]
    ss = [_dot_split(parts, _block_diag(parts, keep_left, keep_right)) for parts in n_parts]
    power = 2
    while 2 * power < CHUNK:
        s_parts = [_split_bf16(s) for s in ss]
        p_parts = [_split_bf16(p) for p in ps]
        prods = [_dot_split(tuple(jnp.concatenate([pp, sp], axis=0) for pp, sp in zip(pp2, sp2)),
                            _block_diag(sp2, keep_left, keep_right))
                 for pp2, sp2 in zip(p_parts, s_parts)]
        ps = [p + pr[0:CHUNK] for p, pr in zip(ps, prods)]
        ss = [pr[CHUNK:2 * CHUNK] for pr in prods]
        power *= 2
    last = [_dot_split(_split_bf16(p), _block_diag(_split_bf16(s), keep_left, keep_right))
            for p, s in zip(ps, ss)]
    return [p + pr for p, pr in zip(ps, last)]


def _delta_kernel(*refs, seq, hb_count, chunk_unroll, zero_init):
    if zero_init:
        (q_ref, k_ref, v_ref, z_ref, ba_ref, bat_ref, wq_ref, wk_ref, wv_ref, hng_ref,
         y_ref, sout_ref, qs, ks, vs, gates, u_scr, wq_scr, lx_scr, egl_scr, o_scr, sscr) = refs
        s0_ref = None
    else:
        (q_ref, k_ref, v_ref, z_ref, ba_ref, bat_ref, wq_ref, wk_ref, wv_ref, hng_ref, s0_ref,
         y_ref, sout_ref, qs, ks, vs, gates, u_scr, wq_scr, lx_scr, egl_scr, o_scr, sscr) = refs
    nc = seq // CHUNK
    wq_rows = 2 * CHUNK
    lx_rows = CHUNK + HEAD_DIM
    h0 = pl.program_id(1) * hb_count

    row8 = lax.broadcasted_iota(jnp.int32, (SUBLANES, HEAD_DIM), 0)
    lane = lax.broadcasted_iota(jnp.int32, (1, LANES), 1)

    def conv_swish(x_ref, w_ref, cols):
        x = x_ref[:, cols].astype(F32)
        prev = pltpu.roll(x, 1, 0)
        prev = jnp.concatenate([jnp.where(row8 >= 1, prev[0:SUBLANES], 0.0), prev[SUBLANES:]], axis=0)
        nxt = pltpu.roll(x, seq - 1, 0)
        nxt = jnp.concatenate([nxt[0:seq - SUBLANES],
                               jnp.where(row8 < SUBLANES - 1, nxt[seq - SUBLANES:], 0.0)], axis=0)
        return _silu(w_ref[0:1, cols] * prev + w_ref[1:2, cols] * x + w_ref[2:3, cols] * nxt)

    def l2norm(x):
        return x * lax.rsqrt(jnp.sum(x * x, axis=-1, keepdims=True) + EPS)

    ba = ba_ref[...]
    for hb in range(hb_count):
        cols = slice(hb * HEAD_DIM, (hb + 1) * HEAD_DIM)
        qs[hb] = l2norm(conv_swish(q_ref, wq_ref, cols)) * (HEAD_DIM ** -0.5)
        ks[hb] = l2norm(conv_swish(k_ref, wk_ref, cols))
        vs[hb] = conv_swish(v_ref, wv_ref, cols)
        for t in range(4):
            col = jnp.sum(jnp.where(lane == t * N_HEADS + h0 + hb, ba, 0.0), axis=-1, keepdims=True)
            gates[t, hb] = jnp.broadcast_to(col, (seq, LANES))
        for d in range(2):
            if zero_init:
                sscr[d, hb] = jnp.zeros((HEAD_DIM, HEAD_DIM), F32)
            else:
                sscr[d, hb] = s0_ref[d, hb]

    ii = lax.broadcasted_iota(jnp.int32, (CHUNK, LANES), 0)
    l64 = lax.broadcasted_iota(jnp.int32, (CHUNK, LANES), 1)
    left = l64 < CHUNK
    right = l64 >= CHUNK
    jj = l64 & (CHUNK - 1)
    below = jnp.where(left, ii - jj, jj - ii)
    incl = below >= 0
    strict = below > 0
    eye2 = jnp.where(ii == jj, 1.0, 0.0).astype(F32)
    keep_left = jnp.where(left, 1.0, 0.0).astype(BF16)
    keep_right = jnp.where(left, 0.0, 1.0).astype(BF16)
    lane_t = lax.broadcasted_iota(jnp.int32, (HEAD_DIM, LANES), 1)
    left_t = lane_t < CHUNK
    right_t = lane_t >= CHUNK
    zeros = jnp.zeros((CHUNK, HEAD_DIM), F32)

    def chunk_products(problems):
        kqs = [_dot_nt(jnp.concatenate([k, q], axis=0), jnp.concatenate([k, k], axis=0))
               for q, k, *_ in problems]
        decays = []
        n2s = []
        for (q, k, v, bf, bb, gf, gb, grow2), kq in zip(problems, kqs):
            diff = jnp.where(left, gf, gb) - grow2
            decay = jnp.where(incl, jnp.exp(jnp.where(incl, diff, 0.0)), 0.0)
            decays.append(decay)
            n2s.append(jnp.where(strict, jnp.where(left, bf, bb) * kq[0:CHUNK] * decay, 0.0))
        tinvs = _unit_triangular_inverse_pairs(n2s, eye2, keep_left, keep_right)
        rhss = []
        for q, k, v, bf, bb, gf, gb, grow2 in problems:
            rhss.append(jnp.concatenate(
                [jnp.concatenate([v * bf, k * bf * jnp.exp(gf), zeros, zeros], axis=1),
                 jnp.concatenate([zeros, zeros, v * bb, k * bb * jnp.exp(gb)], axis=1)], axis=0))
        uws = [_dot(tinv, rhs) for tinv, rhs in zip(tinvs, rhss)]
        results = []
        for (q, k, v, bf, bb, gf, gb, grow2), kq, decay, uw in zip(problems, kqs, decays, uws):
            glf = gf[CHUNK - 1:CHUNK, :]
            glb = gb[0:1, :]
            kt2 = jnp.concatenate([k * jnp.exp(glf - gf), k * jnp.exp(glb - gb)], axis=0).T
            qkm = kq[CHUNK:2 * CHUNK] * decay
            out = []
            for d, (g, gl) in enumerate(((gf, glf), (gb, glb))):
                keep = left if d == 0 else right
                keep_t = left_t if d == 0 else right_t
                u = uw[:, (2 * d) * HEAD_DIM:(2 * d + 1) * HEAD_DIM]
                w = uw[:, (2 * d + 1) * HEAD_DIM:(2 * d + 2) * HEAD_DIM]
                wq = jnp.concatenate([w, q * jnp.exp(g)], axis=0).astype(BF16)
                lx = jnp.concatenate([jnp.where(keep, qkm, 0.0), jnp.where(keep_t, kt2, 0.0)],
                                     axis=0).astype(BF16)
                out.append((u, wq, lx, jnp.broadcast_to(jnp.exp(gl), (8, LANES))))
            results.append(out)
        return results

    def phase1(i, carry):
        where = []
        problems = []
        for j in range(chunk_unroll):
            c = i * chunk_unroll + j
            rows = pl.ds(pl.multiple_of(c * CHUNK, CHUNK), CHUNK)
            for hb in range(hb_count):
                where.append((c, rows, hb))
                problems.append((qs[hb, rows, :], ks[hb, rows, :], vs[hb, rows, :],
                                 gates[0, hb, rows, :], gates[1, hb, rows, :],
                                 gates[2, hb, rows, :], gates[3, hb, rows, :],
                                 bat_ref[c, h0 + hb]))
        for (c, rows, hb), res in zip(where, chunk_products(problems)):
            for d, (u, wq, lx, egl) in enumerate(res):
                u_scr[d, hb, rows, :] = u
                wq_scr[d, hb, pl.ds(pl.multiple_of(c * wq_rows, wq_rows), wq_rows), :] = wq
                lx_scr[d, hb, pl.ds(pl.multiple_of(c * lx_rows, CHUNK), lx_rows), :] = lx
                egl_scr[d, hb, c] = egl
        return carry

    lax.fori_loop(0, nc // chunk_unroll, phase1, 0)

    def phase2(c, carry):
        work = []
        for hb in range(hb_count):
            for d in range(2):
                cd = c if d == 0 else nc - 1 - c
                rows = pl.ds(pl.multiple_of(cd * CHUNK, CHUNK), CHUNK)
                work.append((hb, d, rows, sscr[d, hb], u_scr[d, hb, rows, :],
                             wq_scr[d, hb, pl.ds(pl.multiple_of(cd * wq_rows, wq_rows), wq_rows), :],
                             lx_scr[d, hb, pl.ds(pl.multiple_of(cd * lx_rows, CHUNK), lx_rows), :],
                             egl_scr[d, hb, cd]))
        wss = [jnp.dot(wq, state.astype(BF16), preferred_element_type=F32)
               for _, _, _, state, _, wq, _, _ in work]
        v_news = [(u - ws[0:CHUNK]).astype(BF16)
                  for (_, _, _, _, u, _, _, _), ws in zip(work, wss)]
        oms = [jnp.dot(lx, jnp.concatenate([v_new, v_new], axis=0), preferred_element_type=F32)
               for (_, _, _, _, _, _, lx, _), v_new in zip(work, v_news)]
        for (hb, d, rows, state, _, _, _, egl), ws, om in zip(work, wss, oms):
            o_scr[d, hb, rows, :] = ws[CHUNK:2 * CHUNK] + om[0:CHUNK]
            sscr[d, hb] = state * egl[0:1, :] + om[CHUNK:lx_rows]
        return carry

    lax.fori_loop(0, nc, phase2, 0)

    for hb in range(hb_count):
        cols = slice(hb * HEAD_DIM, (hb + 1) * HEAD_DIM)
        o = o_scr[0, hb] + o_scr[1, hb]
        o = o * lax.rsqrt(jnp.mean(o * o, axis=-1, keepdims=True) + EPS) * hng_ref[...]
        y_ref[:, cols] = (o * _silu(z_ref[:, cols].astype(F32))).astype(y_ref.dtype)
        for d in range(2):
            sout_ref[d, hb] = sscr[d, hb]


def _delta_call(proj, ba, bat, conv_qkv_w, head_norm_g, s0, batch, seq, col_q):
    nc = seq // CHUNK
    hb_count = N_HEADS if seq <= 256 else N_HEADS // 2
    chunk_unroll = PHASE1_CHUNKS
    n_hblk = N_HEADS // hb_count
    bw = hb_count * HEAD_DIM
    kd = N_HEADS * HEAD_DIM
    proj3 = proj.reshape(batch, seq, proj.shape[1])
    ba3 = ba.reshape(batch, seq, LANES)
    cq = col_q // hb_count
    tok = lambda k: pl.BlockSpec((None, seq, bw), lambda b, h: (b, 0, cq + k * n_hblk + h))
    cw = lambda k: pl.BlockSpec((3, bw), lambda b, h: (0, k * n_hblk + h))
    in_specs = [tok(0), tok(1), tok(2), tok(3),
                pl.BlockSpec((None, seq, LANES), lambda b, h: (b, 0, 0)),
                pl.BlockSpec((nc, N_HEADS, 1, LANES), lambda b, h: (b, 0, 0, 0)),
                cw(0), cw(1), cw(2),
                pl.BlockSpec((1, HEAD_DIM), lambda b, h: (0, 0))]
    args = [proj3, proj3, proj3, proj3, ba3, bat, conv_qkv_w, conv_qkv_w, conv_qkv_w,
            head_norm_g.reshape(1, HEAD_DIM)]
    state_spec = pl.BlockSpec((None, 2, hb_count, HEAD_DIM, HEAD_DIM), lambda b, h: (b, 0, h, 0, 0))
    if s0 is not None:
        states, layer = s0
        in_specs.append(pl.BlockSpec((None, None, 2, hb_count, HEAD_DIM, HEAD_DIM),
                                     lambda b, h: (b, layer, 0, h, 0, 0)))
        args.append(states)
    head_scr = pltpu.VMEM((hb_count, seq, HEAD_DIM), F32)
    dir_scr = pltpu.VMEM((2, hb_count, seq, HEAD_DIM), F32)
    y, s_out = pl.pallas_call(
        functools.partial(_delta_kernel, seq=seq, hb_count=hb_count, chunk_unroll=chunk_unroll,
                          zero_init=s0 is None),
        grid=(batch, n_hblk),
        in_specs=in_specs,
        out_specs=[pl.BlockSpec((None, seq, bw), lambda b, h: (b, 0, h)), state_spec],
        out_shape=[jax.ShapeDtypeStruct((batch, seq, kd), BF16),
                   jax.ShapeDtypeStruct((batch, 2, N_HEADS, HEAD_DIM, HEAD_DIM), F32)],
        scratch_shapes=[head_scr, head_scr, head_scr,
                        pltpu.VMEM((4, hb_count, seq, LANES), F32),
                        dir_scr,
                        pltpu.VMEM((2, hb_count, nc * 2 * CHUNK, HEAD_DIM), BF16),
                        pltpu.VMEM((2, hb_count, nc * (CHUNK + HEAD_DIM), LANES), BF16),
                        pltpu.VMEM((2, hb_count, nc, 8, LANES), F32),
                        dir_scr,
                        pltpu.VMEM((2, hb_count, HEAD_DIM, HEAD_DIM), F32)],
        compiler_params=_params(("parallel", "parallel")),
        name="delta",
    )(*args)
    return y.reshape(batch * seq, kd), s_out


def _out_proj_kernel(x_ref, ya_ref, yb_ref, ga_ref, gb_ref, mod_ref, wpa_ref, wpb_ref, wo_ref,
                     fg_ref, o_ref, *, d, final):
    gate = mod_ref[:, 2 * d:3 * d]
    out_a = jnp.dot(ya_ref[...], wpa_ref[...], preferred_element_type=F32)
    out_b = jnp.dot(yb_ref[...], wpb_ref[...], preferred_element_type=F32)
    merged = (_sigmoid(ga_ref[...].astype(F32)) * out_a
              + _sigmoid(gb_ref[...].astype(F32)) * out_b)
    x = x_ref[...] + gate * jnp.dot(merged.astype(BF16), wo_ref[...], preferred_element_type=F32)
    if final:
        x = x * lax.rsqrt(jnp.mean(x * x, axis=-1, keepdims=True) + EPS) * fg_ref[...]
    o_ref[...] = x


def _out_proj_call(x, ya, yb, proj, mod, w_pa, w_pb, w_o, final_g, layer, tokens_per_mod, final):
    n, d = x.shape
    cc = ya.shape[1]
    tm = 512
    kern = functools.partial(_out_proj_kernel, d=d, final=final)
    resident = lambda shape: pl.BlockSpec((None,) + tuple(shape[1:]), lambda i: (layer, 0, 0),
                                          pipeline_mode=pl.Buffered(1))
    return pl.pallas_call(
        kern,
        grid=(n // tm,),
        in_specs=[pl.BlockSpec((tm, d), lambda i: (i, 0)),
                  pl.BlockSpec((tm, cc), lambda i: (i, 0)),
                  pl.BlockSpec((tm, cc), lambda i: (i, 0)),
                  pl.BlockSpec((tm, d), lambda i: (i, 0)),
                  pl.BlockSpec((tm, d), lambda i: (i, 1)),
                  pl.BlockSpec((None, 1, 3 * d), lambda i: (_mod_row(i, tm, tokens_per_mod), 0, 0)),
                  resident(w_pa.shape), resident(w_pb.shape), resident(w_o.shape),
                  pl.BlockSpec((1, d), lambda i: (0, 0))],
        out_specs=pl.BlockSpec((tm, d), lambda i: (i, 0)),
        out_shape=jax.ShapeDtypeStruct((n, d), F32),
        compiler_params=_params(("parallel",)),
        name="out_proj",
    )(x, ya, yb, proj, proj, mod, w_pa, w_pb, w_o, final_g)


def kernel(x_prompt, x_sample, state_delta, c, c_ctx, w_mod, b_mod, norm_g, w_in, conv_a_w,
           conv_a_b, ln_a_g, ln_a_b, w_pa, conv_qkv_w, a_log, dt_bias, head_norm_g, w_pb, w_o,
           final_norm_g):
    batch, seq, d = x_prompt.shape
    dec_batch, dec_seq, _ = x_sample.shape
    depth = w_mod.shape[0]
    cc = conv_a_w.shape[2]
    kd = N_HEADS * HEAD_DIM
    o_ba = 3 * cc + 4 * kd
    o_gate = o_ba + 4 * N_HEADS

    cv = jnp.concatenate([c_ctx[None, :], c, jnp.zeros((MOD_ROWS - 1 - dec_batch, d), F32)], axis=0)
    mod = _mod_call(cv, w_mod, b_mod).reshape(depth, MOD_ROWS, 1, 3 * d)

    w_t = jnp.swapaxes(w_in, 1, 2)
    gate_pad = ((0, 0), (2 * N_HEADS, LANES - 4 * N_HEADS))
    alog_rows = jnp.pad(a_log.reshape(depth, 2 * N_HEADS), gate_pad).reshape(depth, 1, LANES)
    dtb_rows = jnp.pad(dt_bias.reshape(depth, 2 * N_HEADS), gate_pad).reshape(depth, 1, LANES)
    w_pa16, w_pb16, w_o16 = w_pa.astype(BF16), w_pb.astype(BF16), w_o.astype(BF16)
    col_conv = (2 * d) // cc
    col_q = (2 * d + 3 * cc) // HEAD_DIM
    final_g = final_norm_g.reshape(1, d)

    def layer(x, l, latent):
        b, s = (dec_batch, dec_seq) if latent else (batch, seq)
        tokens_per_mod = dec_seq if latent else None
        proj, ba, bat = _in_proj_call(x, mod[l], norm_g[l].reshape(1, d), w_t, alog_rows[l],
                                      dtb_rows[l], l, o_ba, o_gate - o_ba, tokens_per_mod)
        ya = _conv_call(proj, conv_a_w[l], conv_a_b[l], ln_a_g[l], ln_a_b[l], b, s, col_conv,
                        axial=latent)
        s0 = (state_delta, l) if latent else None
        yb, s_out = _delta_call(proj, ba, bat, conv_qkv_w[l], head_norm_g[l], s0, b, s, col_q)
        x = _out_proj_call(x, ya, yb, proj, mod[l], w_pa16, w_pb16, w_o16, final_g, l,
                           tokens_per_mod, final=(l == depth - 1))
        return x, s_out

    h = x_prompt.reshape(batch * seq, d)
    hs = x_sample.reshape(dec_batch * dec_seq, d)
    ctx_states = []
    for l in range(depth):
        h, st = layer(h, l, latent=False)
        ctx_states.append(st)
        hs, _ = layer(hs, l, latent=True)
    y_prompt = h.reshape(batch, seq, d)
    y_sample = hs.reshape(dec_batch, dec_seq, d)
    state_delta_new = jnp.stack(ctx_states, axis=1)
    return (y_prompt, y_sample, state_delta_new)
```

```python
import functools

import jax
import jax.numpy as jnp
from jax import lax
from jax.experimental import pallas as pl
from jax.experimental.pallas import tpu as pltpu

F32 = jnp.float32
BF16 = jnp.bfloat16

EPS = 1e-6
GRID_W = 64
CHUNK = 64
N_HEADS = 8
HEAD_DIM = 128
LANES = 128
SUBLANES = 8
CONV_LEAD = 16
MOD_ROWS = 8
PHASE1_CHUNKS = 4
V7X_VMEM_LIMIT_BYTES = 58 * 1024 * 1024


def _sigmoid(x):
    return jax.nn.sigmoid(x)


def _silu(x):
    return x * jax.nn.sigmoid(x)


def _params(semantics):
    return pltpu.CompilerParams(dimension_semantics=semantics,
                                vmem_limit_bytes=V7X_VMEM_LIMIT_BYTES)


def _dot(a, b):
    return jnp.dot(a.astype(BF16), b.astype(BF16), preferred_element_type=F32)


def _dot_nt(a, b):
    return lax.dot_general(a.astype(BF16), b.astype(BF16), (((1,), (1,)), ((), ())),
                           preferred_element_type=F32)


def _mod_kernel(cv_ref, w_ref, b_ref, o_ref):
    a = _silu(cv_ref[...])
    o_ref[...] = _dot(a, w_ref[...]) + b_ref[...]


def _mod_call(cv, w_mod, b_mod):
    depth, d, d3 = w_mod.shape
    tn = 1024
    return pl.pallas_call(
        _mod_kernel,
        grid=(depth, d3 // tn),
        in_specs=[pl.BlockSpec((MOD_ROWS, d), lambda l, j: (0, 0)),
                  pl.BlockSpec((None, d, tn), lambda l, j: (l, 0, j)),
                  pl.BlockSpec((None, 1, tn), lambda l, j: (l, 0, j))],
        out_specs=pl.BlockSpec((None, MOD_ROWS, tn), lambda l, j: (l, 0, j)),
        out_shape=jax.ShapeDtypeStruct((depth, MOD_ROWS, d3), F32),
        compiler_params=_params(("parallel", "parallel")),
        name="mod",
    )(cv, w_mod, b_mod.reshape(depth, 1, d3))


def _mod_row(i, tm, tokens_per_mod):
    if tokens_per_mod is None:
        return 0
    return 1 + (i * tm) // tokens_per_mod


def _in_proj_kernel(x_ref, mod_ref, ng_ref, w_ref, wba_ref, alog_ref, dtb_ref,
                    proj_ref, ba_ref, bat_ref, hn_ref, wbf_ref, *, tm, d, tokens_per_mod):
    j = pl.program_id(0)
    i = pl.program_id(1)
    base = i * tm
    tile = pl.ds(pl.multiple_of(base, tm), tm)
    span = tm if tokens_per_mod is None else min(tm, tokens_per_mod)

    @pl.when(j == 0)
    def _():
        table = mod_ref[...]
        table_row = lax.broadcasted_iota(jnp.int32, table.shape, 0)
        rb = 64
        for part in range(tm // span):
            r = _mod_row(i * (tm // span) + part, span, tokens_per_mod)
            vec = jnp.sum(jnp.where(table_row == r, table, 0.0), axis=0, keepdims=True)
            shift = vec[:, 0:d]
            gain = ng_ref[...] * (1.0 + vec[:, d:2 * d])

            def norm_rows(t, carry, shift=shift, gain=gain, first=part * span):
                x = x_ref[pl.ds(pl.multiple_of(first + t * rb, rb), rb), :]
                inv = lax.rsqrt(jnp.mean(x * x, axis=-1, keepdims=True) + EPS)
                hn_ref[pl.ds(pl.multiple_of(base + first + t * rb, rb), rb), :] = (
                    x * inv * gain + shift).astype(BF16)
                return carry

            lax.fori_loop(0, span // rb, norm_rows, 0)

        pba = _dot_nt(hn_ref[tile, :], wba_ref[0])
        beta = _sigmoid(pba)
        z = pba + dtb_ref[...]
        softplus = jnp.maximum(z, 0.0) + jnp.log(1.0 + jnp.exp(-jnp.abs(z)))
        g = -jnp.exp(alog_ref[...]) * softplus
        pos = lax.broadcasted_iota(jnp.int32, (tm, LANES), 0) & (CHUNK - 1)
        pre = g
        suf = g
        s = 1
        while s < CHUNK:
            pre = pre + jnp.where(pos >= s, pltpu.roll(pre, s, 0), 0.0)
            suf = suf + jnp.where(pos < CHUNK - s, pltpu.roll(suf, tm - s, 0), 0.0)
            s *= 2
        lane = lax.broadcasted_iota(jnp.int32, (tm, LANES), 1)
        ba_ref[...] = jnp.where(lane < 2 * N_HEADS, beta,
                                jnp.where(lane < 3 * N_HEADS, pre,
                                          jnp.where(lane < 4 * N_HEADS, suf, 0.0)))
        fwd_half = lax.broadcasted_iota(jnp.int32, (N_HEADS, LANES), 1) < CHUNK
        for c in range(tm // CHUNK):
            rows = slice(c * CHUNK, (c + 1) * CHUNK)
            by_gate = jnp.concatenate([pre[rows], suf[rows]], axis=0).T
            both = jnp.where(fwd_half, by_gate[2 * N_HEADS:3 * N_HEADS],
                             by_gate[3 * N_HEADS:4 * N_HEADS])
            for h in range(N_HEADS):
                bat_ref[c, h] = both[h:h + 1, :]

    @pl.when(i == 0)
    def _():
        wbf_ref[...] = w_ref[0].astype(BF16)

    proj_ref[...] = _dot_nt(hn_ref[tile, :], wbf_ref[...]).astype(proj_ref.dtype)


def _in_proj_call(x, mod, norm_g, w_t, alog_row, dtb_row, layer, main_width, skip, tokens_per_mod):
    n, d = x.shape
    gate_start = main_width + skip
    gate_width = w_t.shape[1] - gate_start
    tm, tn = 2048, 512
    n_tok = n // tm
    n_main = main_width // tn
    n_gate = gate_width // tn
    tok_tile = lambda j, i: jnp.where(j == 0, i, n_tok - 1)
    assert gate_start % SUBLANES == 0
    w_row = lambda j: pl.multiple_of(
        jnp.where(j < n_main, j * tn, gate_start + (j - n_main) * tn), SUBLANES)
    return pl.pallas_call(
        functools.partial(_in_proj_kernel, tm=tm, d=d, tokens_per_mod=tokens_per_mod),
        grid=(n_main + n_gate, n_tok),
        in_specs=[pl.BlockSpec((tm, d), lambda j, i: (tok_tile(j, i), 0),
                               pipeline_mode=pl.Buffered(1)),
                  pl.BlockSpec(mod.shape, lambda j, i: (0, 0)),
                  pl.BlockSpec((1, d), lambda j, i: (0, 0)),
                  pl.BlockSpec((pl.Element(1), pl.Element(tn), pl.Element(d)),
                               lambda j, i: (layer, w_row(j), 0)),
                  pl.BlockSpec((pl.Element(1), pl.Element(LANES), pl.Element(d)),
                               lambda j, i: (layer, main_width, 0)),
                  pl.BlockSpec((1, LANES), lambda j, i: (0, 0)),
                  pl.BlockSpec((1, LANES), lambda j, i: (0, 0))],
        out_specs=[pl.BlockSpec((tm, tn),
                                lambda j, i: (i, jnp.where(j < n_main, j + n_gate, j - n_main))),
                   pl.BlockSpec((tm, LANES), lambda j, i: (tok_tile(j, i), 0)),
                   pl.BlockSpec((tm // CHUNK, N_HEADS, 1, LANES),
                                lambda j, i: (tok_tile(j, i), 0, 0, 0))],
        out_shape=[jax.ShapeDtypeStruct((n, gate_width + main_width), BF16),
                   jax.ShapeDtypeStruct((n, LANES), F32),
                   jax.ShapeDtypeStruct((n // CHUNK, N_HEADS, 1, LANES), F32)],
        scratch_shapes=[pltpu.VMEM((n, d), BF16), pltpu.VMEM((tn, d), BF16)],
        compiler_params=_params(("arbitrary", "arbitrary")),
        name="in_proj",
    )(x, mod, norm_g, w_t, w_t, alog_row, dtb_row)


def _ln_swish_gate(conv_ref, az_ref, g_ref, b_ref, o_ref, n_rows):
    rb = 16
    blocks = 8
    gain = g_ref[...]
    bias = b_ref[...]

    def rows_body(t, carry):
        rows = [pl.ds(pl.multiple_of((t * blocks + u) * rb, rb), rb) for u in range(blocks)]
        zs = [conv_ref[r, :] for r in rows]
        mus = [jnp.mean(z, axis=-1, keepdims=True) for z in zs]
        zcs = [z - mu for z, mu in zip(zs, mus)]
        variances = [jnp.mean(zc * zc, axis=-1, keepdims=True) for zc in zcs]
        for r, zc, var in zip(rows, zcs, variances):
            y = _silu(zc * lax.rsqrt(var + EPS) * gain + bias)
            o_ref[r, :] = (y * _silu(az_ref[r, :].astype(F32))).astype(o_ref.dtype)
        return carry

    lax.fori_loop(0, n_rows // (rb * blocks), rows_body, 0)


def _conv_seq_kernel(av_ref, ag_ref, az_ref, w_ref, cb_ref, g_ref, b_ref, o_ref,
                     pad_ref, shift_ref, conv_ref, *, seq, taps):
    half = taps // 2
    lead = CONV_LEAD
    ch = av_ref.shape[1]
    pad_ref[0:lead, :] = jnp.zeros((lead, ch), F32)
    pad_ref[lead + seq:lead + seq + lead, :] = jnp.zeros((lead, ch), F32)
    pad_ref[lead:lead + seq, :] = av_ref[...].astype(F32) * _sigmoid(ag_ref[...].astype(F32))
    rb = 128
    shifted_rows = seq + 2 * lead - SUBLANES

    def lane_block(c, carry):
        lanes = pl.ds(pl.multiple_of(c * LANES, LANES), LANES)
        for s in range(1, SUBLANES):
            shift_ref[s - 1, 0:shifted_rows, :] = pad_ref[pl.ds(s, shifted_rows), lanes]
        for r in range(seq // rb):
            acc = jnp.zeros((rb, LANES), F32)
            for j in range(taps):
                s = (lead - half + j) % SUBLANES
                base = r * rb + lead - half + j - s
                if s == 0:
                    src = pad_ref[pl.ds(base, rb), lanes]
                else:
                    src = shift_ref[s - 1, pl.ds(base, rb), :]
                acc = acc + w_ref[j:j + 1, lanes] * src
            conv_ref[r * rb:(r + 1) * rb, lanes] = acc + cb_ref[:, lanes]
        return carry

    lax.fori_loop(0, ch // LANES, lane_block, 0)
    _ln_swish_gate(conv_ref, az_ref, g_ref, b_ref, o_ref, seq)


def _conv_axial_kernel(av_ref, ag_ref, az_ref, w_ref, cb_ref, g_ref, b_ref, o_ref,
                       pad_ref, shift_ref, ver_ref, conv_ref, *, grid_h, taps):
    half = taps // 2
    lead = CONV_LEAD
    shifted_rows = GRID_W + 2 * lead - SUBLANES
    ch = av_ref.shape[1]
    hc = ch // 2
    n_tok = grid_h * GRID_W
    for r in range(grid_h):
        rows = slice(r * GRID_W, (r + 1) * GRID_W)
        ua = av_ref[rows, :].astype(F32) * _sigmoid(ag_ref[rows, :].astype(F32))
        pad_ref[r, 0:lead, :] = jnp.zeros((lead, hc), F32)
        pad_ref[r, lead + GRID_W:lead + GRID_W + lead, :] = jnp.zeros((lead, hc), F32)
        pad_ref[r, lead:lead + GRID_W, :] = ua[:, 0:hc]
        ver_ref[rows, :] = ua[:, hc:ch]

    def lane_block(c, carry):
        lanes = pl.ds(pl.multiple_of(c * LANES, LANES), LANES)
        lanes_v = pl.ds(pl.multiple_of(hc + c * LANES, LANES), LANES)

        def grid_row(r, carry2):
            out_rows = pl.ds(pl.multiple_of(r * GRID_W, GRID_W), GRID_W)
            for s in range(1, SUBLANES):
                shift_ref[s - 1] = pad_ref[r, pl.ds(s, shifted_rows), lanes]
            acc = jnp.zeros((GRID_W, LANES), F32)
            for j in range(taps):
                s = (lead - half + j) % SUBLANES
                base = lead - half + j - s
                if s == 0:
                    src = pad_ref[r, pl.ds(base, GRID_W), lanes]
                else:
                    src = shift_ref[s - 1, pl.ds(base, GRID_W), :]
                acc = acc + w_ref[j:j + 1, lanes] * src
            conv_ref[out_rows, lanes] = acc + cb_ref[:, lanes]
            return carry2

        lax.fori_loop(0, grid_h, grid_row, 0)
        for r in range(grid_h):
            acc = jnp.zeros((GRID_W, LANES), F32)
            for src in range(grid_h):
                tap = src + half - r
                acc = acc + (w_ref[tap:tap + 1, lanes_v]
                             * ver_ref[src * GRID_W:(src + 1) * GRID_W, lanes])
            conv_ref[r * GRID_W:(r + 1) * GRID_W, lanes_v] = acc + cb_ref[:, lanes_v]
        return carry

    lax.fori_loop(0, hc // LANES, lane_block, 0)
    _ln_swish_gate(conv_ref, az_ref, g_ref, b_ref, o_ref, n_tok)


def _conv_call(proj, conv_w, conv_b, ln_g, ln_b, batch, seq, col0, axial):
    taps, ch = conv_w.shape
    proj3 = proj.reshape(batch, seq, proj.shape[1])
    row = lambda v: v.reshape(1, ch)
    if axial:
        grid_h = seq // GRID_W
        assert taps // 2 >= grid_h - 1, "every grid row must lie inside the conv window"
        kern = functools.partial(_conv_axial_kernel, grid_h=grid_h, taps=taps)
        scratch = [pltpu.VMEM((grid_h, GRID_W + 2 * CONV_LEAD, ch // 2), F32),
                   pltpu.VMEM((SUBLANES - 1, GRID_W + 2 * CONV_LEAD - SUBLANES, LANES), F32),
                   pltpu.VMEM((seq, ch // 2), F32),
                   pltpu.VMEM((seq, ch), F32)]
    else:
        kern = functools.partial(_conv_seq_kernel, seq=seq, taps=taps)
        scratch = [pltpu.VMEM((seq + 2 * CONV_LEAD, ch), F32),
                   pltpu.VMEM((SUBLANES - 1, seq + 2 * CONV_LEAD - SUBLANES, LANES), F32),
                   pltpu.VMEM((seq, ch), F32)]
    assert taps // 2 < CONV_LEAD
    tok = lambda k: pl.BlockSpec((None, seq, ch), lambda b: (b, 0, col0 + k))
    const = lambda shape: pl.BlockSpec(shape, lambda b: (0, 0))
    out = pl.pallas_call(
        kern,
        grid=(batch,),
        in_specs=[tok(0), tok(1), tok(2), const((taps, ch)), const((1, ch)), const((1, ch)),
                  const((1, ch))],
        out_specs=pl.BlockSpec((None, seq, ch), lambda b: (b, 0, 0)),
        out_shape=jax.ShapeDtypeStruct((batch, seq, ch), BF16),
        scratch_shapes=scratch,
        compiler_params=_params(("parallel",)),
        name="conv_axial" if axial else "conv_seq",
    )(proj3, proj3, proj3, conv_w, row(conv_b), row(ln_g), row(ln_b))
    return out.reshape(batch * seq, ch)


def _split_bf16(x):
    hi = x.astype(BF16)
    return hi, (x - hi.astype(F32)).astype(BF16)


def _dot_split(a_parts, b_parts):
    (a_hi, a_lo), (b_hi, b_lo) = a_parts, b_parts
    n = b_hi.shape[1]
    rhs = jnp.concatenate([jnp.concatenate([b_hi, b_lo], axis=1),
                           jnp.concatenate([b_hi, jnp.zeros_like(b_lo)], axis=1)], axis=0)
    both = jnp.dot(jnp.concatenate([a_hi, a_lo], axis=1), rhs, preferred_element_type=F32)
    return both[:, 0:n] + both[:, n:2 * n]


def _block_diag(parts, keep_left, keep_right):
    return tuple(jnp.concatenate([x * keep_left, x * keep_right], axis=0) for x in parts)


def _unit_triangular_inverse_pairs(n2s, eye2, keep_left, keep_right):
    ps = [eye2 - n2 for n2 in n2s]
    n_parts = [_split_bf16(n2) for n2 in n2s]
    ss = [_dot_split(parts, _block_diag(parts, keep_left, keep_right)) for parts in n_parts]
    power = 2
    while 2 * power < CHUNK:
        s_parts = [_split_bf16(s) for s in ss]
        p_parts = [_split_bf16(p) for p in ps]
        prods = [_dot_split(tuple(jnp.concatenate([pp, sp], axis=0) for pp, sp in zip(pp2, sp2)),
                            _block_diag(sp2, keep_left, keep_right))
                 for pp2, sp2 in zip(p_parts, s_parts)]
        ps = [p + pr[0:CHUNK] for p, pr in zip(ps, prods)]
        ss = [pr[CHUNK:2 * CHUNK] for pr in prods]
        power *= 2
    last = [_dot_split(_split_bf16(p), _block_diag(_split_bf16(s), keep_left, keep_right))
            for p, s in zip(ps, ss)]
    return [p + pr for p, pr in zip(ps, last)]


def _delta_kernel(*refs, seq, hb_count, chunk_unroll, zero_init):
    if zero_init:
        (q_ref, k_ref, v_ref, z_ref, ba_ref, bat_ref, wq_ref, wk_ref, wv_ref, hng_ref,
         y_ref, sout_ref, qs, ks, vs, gates, u_scr, wq_scr, lx_scr, egl_scr, o_scr, sscr) = refs
        s0_ref = None
    else:
        (q_ref, k_ref, v_ref, z_ref, ba_ref, bat_ref, wq_ref, wk_ref, wv_ref, hng_ref, s0_ref,
         y_ref, sout_ref, qs, ks, vs, gates, u_scr, wq_scr, lx_scr, egl_scr, o_scr, sscr) = refs
    nc = seq // CHUNK
    wq_rows = 2 * CHUNK
    lx_rows = CHUNK + HEAD_DIM
    h0 = pl.program_id(1) * hb_count

    row8 = lax.broadcasted_iota(jnp.int32, (SUBLANES, HEAD_DIM), 0)
    lane = lax.broadcasted_iota(jnp.int32, (1, LANES), 1)

    def conv_swish(x_ref, w_ref, cols):
        x = x_ref[:, cols].astype(F32)
        prev = pltpu.roll(x, 1, 0)
        prev = jnp.concatenate([jnp.where(row8 >= 1, prev[0:SUBLANES], 0.0), prev[SUBLANES:]], axis=0)
        nxt = pltpu.roll(x, seq - 1, 0)
        nxt = jnp.concatenate([nxt[0:seq - SUBLANES],
                               jnp.where(row8 < SUBLANES - 1, nxt[seq - SUBLANES:], 0.0)], axis=0)
        return _silu(w_ref[0:1, cols] * prev + w_ref[1:2, cols] * x + w_ref[2:3, cols] * nxt)

    def l2norm(x):
        return x * lax.rsqrt(jnp.sum(x * x, axis=-1, keepdims=True) + EPS)

    ba = ba_ref[...]
    for hb in range(hb_count):
        cols = slice(hb * HEAD_DIM, (hb + 1) * HEAD_DIM)
        qs[hb] = l2norm(conv_swish(q_ref, wq_ref, cols)) * (HEAD_DIM ** -0.5)
        ks[hb] = l2norm(conv_swish(k_ref, wk_ref, cols))
        vs[hb] = conv_swish(v_ref, wv_ref, cols)
        for t in range(4):
            col = jnp.sum(jnp.where(lane == t * N_HEADS + h0 + hb, ba, 0.0), axis=-1, keepdims=True)
            gates[t, hb] = jnp.broadcast_to(col, (seq, LANES))
        for d in range(2):
            if zero_init:
                sscr[d, hb] = jnp.zeros((HEAD_DIM, HEAD_DIM), F32)
            else:
                sscr[d, hb] = s0_ref[d, hb]

    ii = lax.broadcasted_iota(jnp.int32, (CHUNK, LANES), 0)
    l64 = lax.broadcasted_iota(jnp.int32, (CHUNK, LANES), 1)
    left = l64 < CHUNK
    right = l64 >= CHUNK
    jj = l64 & (CHUNK - 1)
    below = jnp.where(left, ii - jj, jj - ii)
    incl = below >= 0
    strict = below > 0
    eye2 = jnp.where(ii == jj, 1.0, 0.0).astype(F32)
    keep_left = jnp.where(left, 1.0, 0.0).astype(BF16)
    keep_right = jnp.where(left, 0.0, 1.0).astype(BF16)
    lane_t = lax.broadcasted_iota(jnp.int32, (HEAD_DIM, LANES), 1)
    left_t = lane_t < CHUNK
    right_t = lane_t >= CHUNK
    zeros = jnp.zeros((CHUNK, HEAD_DIM), F32)

    def chunk_products(problems):
        kqs = [_dot_nt(jnp.concatenate([k, q], axis=0), jnp.concatenate([k, k], axis=0))
               for q, k, *_ in problems]
        decays = []
        n2s = []
        for (q, k, v, bf, bb, gf, gb, grow2), kq in zip(problems, kqs):
            diff = jnp.where(left, gf, gb) - grow2
            decay = jnp.where(incl, jnp.exp(jnp.where(incl, diff, 0.0)), 0.0)
            decays.append(decay)
            n2s.append(jnp.where(strict, jnp.where(left, bf, bb) * kq[0:CHUNK] * decay, 0.0))
        tinvs = _unit_triangular_inverse_pairs(n2s, eye2, keep_left, keep_right)
        rhss = []
        for q, k, v, bf, bb, gf, gb, grow2 in problems:
            rhss.append(jnp.concatenate(
                [jnp.concatenate([v * bf, k * bf * jnp.exp(gf), zeros, zeros], axis=1),
                 jnp.concatenate([zeros, zeros, v * bb, k * bb * jnp.exp(gb)], axis=1)], axis=0))
        uws = [_dot(tinv, rhs) for tinv, rhs in zip(tinvs, rhss)]
        results = []
        for (q, k, v, bf, bb, gf, gb, grow2), kq, decay, uw in zip(problems, kqs, decays, uws):
            glf = gf[CHUNK - 1:CHUNK, :]
            glb = gb[0:1, :]
            kt2 = jnp.concatenate([k * jnp.exp(glf - gf), k * jnp.exp(glb - gb)], axis=0).T
            qkm = kq[CHUNK:2 * CHUNK] * decay
            out = []
            for d, (g, gl) in enumerate(((gf, glf), (gb, glb))):
                keep = left if d == 0 else right
                keep_t = left_t if d == 0 else right_t
                u = uw[:, (2 * d) * HEAD_DIM:(2 * d + 1) * HEAD_DIM]
                w = uw[:, (2 * d + 1) * HEAD_DIM:(2 * d + 2) * HEAD_DIM]
                wq = jnp.concatenate([w, q * jnp.exp(g)], axis=0).astype(BF16)
                lx = jnp.concatenate([jnp.where(keep, qkm, 0.0), jnp.where(keep_t, kt2, 0.0)],
                                     axis=0).astype(BF16)
                out.append((u, wq, lx, jnp.broadcast_to(jnp.exp(gl), (8, LANES))))
            results.append(out)
        return results

    def phase1(i, carry):
        where = []
        problems = []
        for j in range(chunk_unroll):
            c = i * chunk_unroll + j
            rows = pl.ds(pl.multiple_of(c * CHUNK, CHUNK), CHUNK)
            for hb in range(hb_count):
                where.append((c, rows, hb))
                problems.append((qs[hb, rows, :], ks[hb, rows, :], vs[hb, rows, :],
                                 gates[0, hb, rows, :], gates[1, hb, rows, :],
                                 gates[2, hb, rows, :], gates[3, hb, rows, :],
                                 bat_ref[c, h0 + hb]))
        for (c, rows, hb), res in zip(where, chunk_products(problems)):
            for d, (u, wq, lx, egl) in enumerate(res):
                u_scr[d, hb, rows, :] = u
                wq_scr[d, hb, pl.ds(pl.multiple_of(c * wq_rows, wq_rows), wq_rows), :] = wq
                lx_scr[d, hb, pl.ds(pl.multiple_of(c * lx_rows, CHUNK), lx_rows), :] = lx
                egl_scr[d, hb, c] = egl
        return carry

    lax.fori_loop(0, nc // chunk_unroll, phase1, 0)

    def phase2(c, carry):
        work = []
        for hb in range(hb_count):
            for d in range(2):
                cd = c if d == 0 else nc - 1 - c
                rows = pl.ds(pl.multiple_of(cd * CHUNK, CHUNK), CHUNK)
                work.append((hb, d, rows, sscr[d, hb], u_scr[d, hb, rows, :],
                             wq_scr[d, hb, pl.ds(pl.multiple_of(cd * wq_rows, wq_rows), wq_rows), :],
                             lx_scr[d, hb, pl.ds(pl.multiple_of(cd * lx_rows, CHUNK), lx_rows), :],
                             egl_scr[d, hb, cd]))
        wss = [jnp.dot(wq, state.astype(BF16), preferred_element_type=F32)
               for _, _, _, state, _, wq, _, _ in work]
        v_news = [(u - ws[0:CHUNK]).astype(BF16)
                  for (_, _, _, _, u, _, _, _), ws in zip(work, wss)]
        oms = [jnp.dot(lx, jnp.concatenate([v_new, v_new], axis=0), preferred_element_type=F32)
               for (_, _, _, _, _, _, lx, _), v_new in zip(work, v_news)]
        for (hb, d, rows, state, _, _, _, egl), ws, om in zip(work, wss, oms):
            o_scr[d, hb, rows, :] = ws[CHUNK:2 * CHUNK] + om[0:CHUNK]
            sscr[d, hb] = state * egl[0:1, :] + om[CHUNK:lx_rows]
        return carry

    lax.fori_loop(0, nc, phase2, 0)

    for hb in range(hb_count):
        cols = slice(hb * HEAD_DIM, (hb + 1) * HEAD_DIM)
        o = o_scr[0, hb] + o_scr[1, hb]
        o = o * lax.rsqrt(jnp.mean(o * o, axis=-1, keepdims=True) + EPS) * hng_ref[...]
        y_ref[:, cols] = (o * _silu(z_ref[:, cols].astype(F32))).astype(y_ref.dtype)
        for d in range(2):
            sout_ref[d, hb] = sscr[d, hb]


def _delta_call(proj, ba, bat, conv_qkv_w, head_norm_g, s0, batch, seq, col_q):
    nc = seq // CHUNK
    hb_count = N_HEADS if seq <= 256 else N_HEADS // 2
    chunk_unroll = PHASE1_CHUNKS
    n_hblk = N_HEADS // hb_count
    bw = hb_count * HEAD_DIM
    kd = N_HEADS * HEAD_DIM
    proj3 = proj.reshape(batch, seq, proj.shape[1])
    ba3 = ba.reshape(batch, seq, LANES)
    cq = col_q // hb_count
    tok = lambda k: pl.BlockSpec((None, seq, bw), lambda b, h: (b, 0, cq + k * n_hblk + h))
    cw = lambda k: pl.BlockSpec((3, bw), lambda b, h: (0, k * n_hblk + h))
    in_specs = [tok(0), tok(1), tok(2), tok(3),
                pl.BlockSpec((None, seq, LANES), lambda b, h: (b, 0, 0)),
                pl.BlockSpec((nc, N_HEADS, 1, LANES), lambda b, h: (b, 0, 0, 0)),
                cw(0), cw(1), cw(2),
                pl.BlockSpec((1, HEAD_DIM), lambda b, h: (0, 0))]
    args = [proj3, proj3, proj3, proj3, ba3, bat, conv_qkv_w, conv_qkv_w, conv_qkv_w,
            head_norm_g.reshape(1, HEAD_DIM)]
    state_spec = pl.BlockSpec((None, 2, hb_count, HEAD_DIM, HEAD_DIM), lambda b, h: (b, 0, h, 0, 0))
    if s0 is not None:
        states, layer = s0
        in_specs.append(pl.BlockSpec((None, None, 2, hb_count, HEAD_DIM, HEAD_DIM),
                                     lambda b, h: (b, layer, 0, h, 0, 0)))
        args.append(states)
    head_scr = pltpu.VMEM((hb_count, seq, HEAD_DIM), F32)
    dir_scr = pltpu.VMEM((2, hb_count, seq, HEAD_DIM), F32)
    y, s_out = pl.pallas_call(
        functools.partial(_delta_kernel, seq=seq, hb_count=hb_count, chunk_unroll=chunk_unroll,
                          zero_init=s0 is None),
        grid=(batch, n_hblk),
        in_specs=in_specs,
        out_specs=[pl.BlockSpec((None, seq, bw), lambda b, h: (b, 0, h)), state_spec],
        out_shape=[jax.ShapeDtypeStruct((batch, seq, kd), BF16),
                   jax.ShapeDtypeStruct((batch, 2, N_HEADS, HEAD_DIM, HEAD_DIM), F32)],
        scratch_shapes=[head_scr, head_scr, head_scr,
                        pltpu.VMEM((4, hb_count, seq, LANES), F32),
                        dir_scr,
                        pltpu.VMEM((2, hb_count, nc * 2 * CHUNK, HEAD_DIM), BF16),
                        pltpu.VMEM((2, hb_count, nc * (CHUNK + HEAD_DIM), LANES), BF16),
                        pltpu.VMEM((2, hb_count, nc, 8, LANES), F32),
                        dir_scr,
                        pltpu.VMEM((2, hb_count, HEAD_DIM, HEAD_DIM), F32)],
        compiler_params=_params(("parallel", "parallel")),
        name="delta",
    )(*args)
    return y.reshape(batch * seq, kd), s_out


def _out_proj_kernel(x_ref, ya_ref, yb_ref, ga_ref, gb_ref, mod_ref, wpa_ref, wpb_ref, wo_ref,
                     fg_ref, o_ref, *, d, final):
    gate = mod_ref[:, 2 * d:3 * d]
    out_a = jnp.dot(ya_ref[...], wpa_ref[...], preferred_element_type=F32)
    out_b = jnp.dot(yb_ref[...], wpb_ref[...], preferred_element_type=F32)
    merged = (_sigmoid(ga_ref[...].astype(F32)) * out_a
              + _sigmoid(gb_ref[...].astype(F32)) * out_b)
    x = x_ref[...] + gate * jnp.dot(merged.astype(BF16), wo_ref[...], preferred_element_type=F32)
    if final:
        x = x * lax.rsqrt(jnp.mean(x * x, axis=-1, keepdims=True) + EPS) * fg_ref[...]
    o_ref[...] = x


def _out_proj_call(x, ya, yb, proj, mod, w_pa, w_pb, w_o, final_g, layer, tokens_per_mod, final):
    n, d = x.shape
    cc = ya.shape[1]
    tm = 512
    kern = functools.partial(_out_proj_kernel, d=d, final=final)
    resident = lambda shape: pl.BlockSpec((None,) + tuple(shape[1:]), lambda i: (layer, 0, 0),
                                          pipeline_mode=pl.Buffered(1))
    return pl.pallas_call(
        kern,
        grid=(n // tm,),
        in_specs=[pl.BlockSpec((tm, d), lambda i: (i, 0)),
                  pl.BlockSpec((tm, cc), lambda i: (i, 0)),
                  pl.BlockSpec((tm, cc), lambda i: (i, 0)),
                  pl.BlockSpec((tm, d), lambda i: (i, 0)),
                  pl.BlockSpec((tm, d), lambda i: (i, 1)),
                  pl.BlockSpec((None, 1, 3 * d), lambda i: (_mod_row(i, tm, tokens_per_mod), 0, 0)),
                  resident(w_pa.shape), resident(w_pb.shape), resident(w_o.shape),
                  pl.BlockSpec((1, d), lambda i: (0, 0))],
        out_specs=pl.BlockSpec((tm, d), lambda i: (i, 0)),
        out_shape=jax.ShapeDtypeStruct((n, d), F32),
        compiler_params=_params(("parallel",)),
        name="out_proj",
    )(x, ya, yb, proj, proj, mod, w_pa, w_pb, w_o, final_g)


def kernel(x_prompt, x_sample, state_delta, c, c_ctx, w_mod, b_mod, norm_g, w_in, conv_a_w,
           conv_a_b, ln_a_g, ln_a_b, w_pa, conv_qkv_w, a_log, dt_bias, head_norm_g, w_pb, w_o,
           final_norm_g):
    batch, seq, d = x_prompt.shape
    dec_batch, dec_seq, _ = x_sample.shape
    depth = w_mod.shape[0]
    cc = conv_a_w.shape[2]
    kd = N_HEADS * HEAD_DIM
    o_ba = 3 * cc + 4 * kd
    o_gate = o_ba + 4 * N_HEADS

    cv = jnp.concatenate([c_ctx[None, :], c, jnp.zeros((MOD_ROWS - 1 - dec_batch, d), F32)], axis=0)
    mod2 = _mod_call(cv, w_mod, b_mod)
    mod = mod2.reshape(depth, MOD_ROWS, 1, 3 * d)

    w_t = jnp.swapaxes(w_in, 1, 2)
    gate_pad = ((0, 0), (2 * N_HEADS, LANES - 4 * N_HEADS))
    alog_rows = jnp.pad(a_log.reshape(depth, 2 * N_HEADS), gate_pad).reshape(depth, 1, LANES)
    dtb_rows = jnp.pad(dt_bias.reshape(depth, 2 * N_HEADS), gate_pad).reshape(depth, 1, LANES)
    w_pa16, w_pb16, w_o16 = w_pa.astype(BF16), w_pb.astype(BF16), w_o.astype(BF16)
    col_conv = (2 * d) // cc
    col_q = (2 * d + 3 * cc) // HEAD_DIM
    final_g = final_norm_g.reshape(1, d)

    def layer(x, l, latent):
        b, s = (dec_batch, dec_seq) if latent else (batch, seq)
        tokens_per_mod = dec_seq if latent else None
        proj, ba, bat = _in_proj_call(x, mod2[l], norm_g[l].reshape(1, d), w_t, alog_rows[l],
                                      dtb_rows[l], l, o_ba, o_gate - o_ba, tokens_per_mod)
        ya = _conv_call(proj, conv_a_w[l], conv_a_b[l], ln_a_g[l], ln_a_b[l], b, s, col_conv,
                        axial=latent)
        s0 = (state_delta, l) if latent else None
        yb, s_out = _delta_call(proj, ba, bat, conv_qkv_w[l], head_norm_g[l], s0, b, s, col_q)
        x = _out_proj_call(x, ya, yb, proj, mod[l], w_pa16, w_pb16, w_o16, final_g, l,
                           tokens_per_mod, final=(l == depth - 1))
        return x, s_out

    h = x_prompt.reshape(batch * seq, d)
    hs = x_sample.reshape(dec_batch * dec_seq, d)
    ctx_states = []
    for l in range(depth):
        h, st = layer(h, l, latent=False)
        ctx_states.append(st)
        hs, _ = layer(hs, l, latent=True)
    y_prompt = h.reshape(batch, seq, d)
    y_sample = hs.reshape(dec_batch, dec_seq, d)
    state_delta_new = jnp.stack(ctx_states, axis=1)
    return (y_prompt, y_sample, state_delta_new)
```

```python
import functools

import jax
import jax.numpy as jnp
from jax import lax
from jax.experimental import pallas as pl
from jax.experimental.pallas import tpu as pltpu

F32 = jnp.float32
BF16 = jnp.bfloat16

EPS = 1e-6
GRID_W = 64
CHUNK = 64
N_HEADS = 8
HEAD_DIM = 128
LANES = 128
SUBLANES = 8
CONV_LEAD = 16
MOD_ROWS = 8
PHASE1_CHUNKS = 4
V7X_VMEM_LIMIT_BYTES = 58 * 1024 * 1024


def _sigmoid(x):
    return jax.nn.sigmoid(x)


def _silu(x):
    return x * jax.nn.sigmoid(x)


def _params(semantics):
    return pltpu.CompilerParams(dimension_semantics=semantics,
                                vmem_limit_bytes=V7X_VMEM_LIMIT_BYTES)


def _dot(a, b):
    return jnp.dot(a.astype(BF16), b.astype(BF16), preferred_element_type=F32)


def _dot_nt(a, b):
    return lax.dot_general(a.astype(BF16), b.astype(BF16), (((1,), (1,)), ((), ())),
                           preferred_element_type=F32)


def _mod_kernel(cv_ref, w_ref, b_ref, o_ref):
    a = _silu(cv_ref[...])
    o_ref[...] = _dot(a, w_ref[...]) + b_ref[...]


def _mod_call(cv, w_mod, b_mod):
    depth, d, d3 = w_mod.shape
    tn = 1024
    return pl.pallas_call(
        _mod_kernel,
        grid=(depth, d3 // tn),
        in_specs=[pl.BlockSpec((MOD_ROWS, d), lambda l, j: (0, 0)),
                  pl.BlockSpec((None, d, tn), lambda l, j: (l, 0, j)),
                  pl.BlockSpec((None, 1, tn), lambda l, j: (l, 0, j))],
        out_specs=pl.BlockSpec((None, MOD_ROWS, tn), lambda l, j: (l, 0, j)),
        out_shape=jax.ShapeDtypeStruct((depth, MOD_ROWS, d3), F32),
        compiler_params=_params(("parallel", "parallel")),
        name="mod",
    )(cv, w_mod, b_mod.reshape(depth, 1, d3))


def _mod_row(i, tm, tokens_per_mod):
    if tokens_per_mod is None:
        return 0
    return 1 + (i * tm) // tokens_per_mod


def _in_proj_kernel(x_ref, mod_ref, ng_ref, w_ref, wba_ref, alog_ref, dtb_ref,
                    proj_ref, ba_ref, bat_ref, hn_ref, wbf_ref, *, tm, d, tokens_per_mod):
    j = pl.program_id(0)
    i = pl.program_id(1)
    base = i * tm
    tile = pl.ds(pl.multiple_of(base, tm), tm)
    span = tm if tokens_per_mod is None else min(tm, tokens_per_mod)

    @pl.when(j == 0)
    def _():
        table = mod_ref[...]
        table_row = lax.broadcasted_iota(jnp.int32, table.shape, 0)
        rb = 64
        for part in range(tm // span):
            r = _mod_row(i * (tm // span) + part, span, tokens_per_mod)
            vec = jnp.sum(jnp.where(table_row == r, table, 0.0), axis=0, keepdims=True)
            shift = vec[:, 0:d]
            gain = ng_ref[...] * (1.0 + vec[:, d:2 * d])

            def norm_rows(t, carry, shift=shift, gain=gain, first=part * span):
                x = x_ref[pl.ds(pl.multiple_of(first + t * rb, rb), rb), :]
                inv = lax.rsqrt(jnp.mean(x * x, axis=-1, keepdims=True) + EPS)
                hn_ref[pl.ds(pl.multiple_of(base + first + t * rb, rb), rb), :] = (
                    x * inv * gain + shift).astype(BF16)
                return carry

            lax.fori_loop(0, span // rb, norm_rows, 0)

        pba = _dot_nt(hn_ref[tile, :], wba_ref[0])
        beta = _sigmoid(pba)
        z = pba + dtb_ref[...]
        softplus = jnp.maximum(z, 0.0) + jnp.log(1.0 + jnp.exp(-jnp.abs(z)))
        g = -jnp.exp(alog_ref[...]) * softplus
        pos = lax.broadcasted_iota(jnp.int32, (tm, LANES), 0) & (CHUNK - 1)
        pre = g
        suf = g
        s = 1
        while s < CHUNK:
            pre = pre + jnp.where(pos >= s, pltpu.roll(pre, s, 0), 0.0)
            suf = suf + jnp.where(pos < CHUNK - s, pltpu.roll(suf, tm - s, 0), 0.0)
            s *= 2
        lane = lax.broadcasted_iota(jnp.int32, (tm, LANES), 1)
        ba_ref[...] = jnp.where(lane < 2 * N_HEADS, beta,
                                jnp.where(lane < 3 * N_HEADS, pre,
                                          jnp.where(lane < 4 * N_HEADS, suf, 0.0)))
        fwd_half = lax.broadcasted_iota(jnp.int32, (N_HEADS, LANES), 1) < CHUNK
        for c in range(tm // CHUNK):
            rows = slice(c * CHUNK, (c + 1) * CHUNK)
            by_gate = jnp.concatenate([pre[rows], suf[rows]], axis=0).T
            both = jnp.where(fwd_half, by_gate[2 * N_HEADS:3 * N_HEADS],
                             by_gate[3 * N_HEADS:4 * N_HEADS])
            for h in range(N_HEADS):
                bat_ref[c, h] = both[h:h + 1, :]

    @pl.when(i == 0)
    def _():
        wbf_ref[...] = w_ref[0].astype(BF16)

    proj_ref[...] = _dot_nt(hn_ref[tile, :], wbf_ref[...]).astype(proj_ref.dtype)


def _in_proj_call(x, mod, norm_g, w_t, alog_row, dtb_row, layer, main_width, skip, tokens_per_mod):
    n, d = x.shape
    gate_start = main_width + skip
    gate_width = w_t.shape[1] - gate_start
    tm, tn = 2048, 512
    n_tok = n // tm
    n_main = main_width // tn
    n_gate = gate_width // tn
    tok_tile = lambda j, i: jnp.where(j == 0, i, n_tok - 1)
    assert gate_start % SUBLANES == 0
    w_row = lambda j: pl.multiple_of(
        jnp.where(j < n_main, j * tn, gate_start + (j - n_main) * tn), SUBLANES)
    return pl.pallas_call(
        functools.partial(_in_proj_kernel, tm=tm, d=d, tokens_per_mod=tokens_per_mod),
        grid=(n_main + n_gate, n_tok),
        in_specs=[pl.BlockSpec((tm, d), lambda j, i: (tok_tile(j, i), 0),
                               pipeline_mode=pl.Buffered(1)),
                  pl.BlockSpec(mod.shape, lambda j, i: (0, 0)),
                  pl.BlockSpec((1, d), lambda j, i: (0, 0)),
                  pl.BlockSpec((pl.Element(1), pl.Element(tn), pl.Element(d)),
                               lambda j, i: (layer, w_row(j), 0)),
                  pl.BlockSpec((pl.Element(1), pl.Element(LANES), pl.Element(d)),
                               lambda j, i: (layer, main_width, 0)),
                  pl.BlockSpec((1, LANES), lambda j, i: (0, 0)),
                  pl.BlockSpec((1, LANES), lambda j, i: (0, 0))],
        out_specs=[pl.BlockSpec((tm, tn),
                                lambda j, i: (i, jnp.where(j < n_main, j + n_gate, j - n_main))),
                   pl.BlockSpec((tm, LANES), lambda j, i: (tok_tile(j, i), 0)),
                   pl.BlockSpec((tm // CHUNK, N_HEADS, 1, LANES),
                                lambda j, i: (tok_tile(j, i), 0, 0, 0))],
        out_shape=[jax.ShapeDtypeStruct((n, gate_width + main_width), BF16),
                   jax.ShapeDtypeStruct((n, LANES), F32),
                   jax.ShapeDtypeStruct((n // CHUNK, N_HEADS, 1, LANES), F32)],
        scratch_shapes=[pltpu.VMEM((n, d), BF16), pltpu.VMEM((tn, d), BF16)],
        compiler_params=_params(("arbitrary", "arbitrary")),
        name="in_proj",
    )(x, mod, norm_g, w_t, w_t, alog_row, dtb_row)


def _ln_swish_gate(conv_ref, az_ref, g_ref, b_ref, o_ref, n_rows):
    rb = 16
    blocks = 8
    gain = g_ref[...]
    bias = b_ref[...]

    def rows_body(t, carry):
        rows = [pl.ds(pl.multiple_of((t * blocks + u) * rb, rb), rb) for u in range(blocks)]
        zs = [conv_ref[r, :] for r in rows]
        mus = [jnp.mean(z, axis=-1, keepdims=True) for z in zs]
        zcs = [z - mu for z, mu in zip(zs, mus)]
        variances = [jnp.mean(zc * zc, axis=-1, keepdims=True) for zc in zcs]
        for r, zc, var in zip(rows, zcs, variances):
            y = _silu(zc * lax.rsqrt(var + EPS) * gain + bias)
            o_ref[r, :] = (y * _silu(az_ref[r, :].astype(F32))).astype(o_ref.dtype)
        return carry

    lax.fori_loop(0, n_rows // (rb * blocks), rows_body, 0)


def _conv_seq_kernel(av_ref, ag_ref, az_ref, w_ref, cb_ref, g_ref, b_ref, o_ref,
                     pad_ref, shift_ref, conv_ref, *, seq, taps):
    half = taps // 2
    lead = CONV_LEAD
    ch = av_ref.shape[1]
    pad_ref[0:lead, :] = jnp.zeros((lead, ch), F32)
    pad_ref[lead + seq:lead + seq + lead, :] = jnp.zeros((lead, ch), F32)
    pad_ref[lead:lead + seq, :] = av_ref[...].astype(F32) * _sigmoid(ag_ref[...].astype(F32))
    rb = 128
    shifted_rows = seq + 2 * lead - SUBLANES

    def lane_block(c, carry):
        lanes = pl.ds(pl.multiple_of(c * LANES, LANES), LANES)
        for s in range(1, SUBLANES):
            shift_ref[s - 1, 0:shifted_rows, :] = pad_ref[pl.ds(s, shifted_rows), lanes]
        for r in range(seq // rb):
            acc = jnp.zeros((rb, LANES), F32)
            for j in range(taps):
                s = (lead - half + j) % SUBLANES
                base = r * rb + lead - half + j - s
                if s == 0:
                    src = pad_ref[pl.ds(base, rb), lanes]
                else:
                    src = shift_ref[s - 1, pl.ds(base, rb), :]
                acc = acc + w_ref[j:j + 1, lanes] * src
            conv_ref[r * rb:(r + 1) * rb, lanes] = acc + cb_ref[:, lanes]
        return carry

    lax.fori_loop(0, ch // LANES, lane_block, 0)
    _ln_swish_gate(conv_ref, az_ref, g_ref, b_ref, o_ref, seq)


def _conv_axial_kernel(av_ref, ag_ref, az_ref, w_ref, cb_ref, g_ref, b_ref, o_ref,
                       pad_ref, shift_ref, ver_ref, conv_ref, *, grid_h, taps):
    half = taps // 2
    lead = CONV_LEAD
    shifted_rows = GRID_W + 2 * lead - SUBLANES
    ch = av_ref.shape[1]
    hc = ch // 2
    n_tok = grid_h * GRID_W
    for r in range(grid_h):
        rows = slice(r * GRID_W, (r + 1) * GRID_W)
        ua = av_ref[rows, :].astype(F32) * _sigmoid(ag_ref[rows, :].astype(F32))
        pad_ref[r, 0:lead, :] = jnp.zeros((lead, hc), F32)
        pad_ref[r, lead + GRID_W:lead + GRID_W + lead, :] = jnp.zeros((lead, hc), F32)
        pad_ref[r, lead:lead + GRID_W, :] = ua[:, 0:hc]
        ver_ref[rows, :] = ua[:, hc:ch]

    def lane_block(c, carry):
        lanes = pl.ds(pl.multiple_of(c * LANES, LANES), LANES)
        lanes_v = pl.ds(pl.multiple_of(hc + c * LANES, LANES), LANES)

        def grid_row(r, carry2):
            out_rows = pl.ds(pl.multiple_of(r * GRID_W, GRID_W), GRID_W)
            for s in range(1, SUBLANES):
                shift_ref[s - 1] = pad_ref[r, pl.ds(s, shifted_rows), lanes]
            acc = jnp.zeros((GRID_W, LANES), F32)
            for j in range(taps):
                s = (lead - half + j) % SUBLANES
                base = lead - half + j - s
                if s == 0:
                    src = pad_ref[r, pl.ds(base, GRID_W), lanes]
                else:
                    src = shift_ref[s - 1, pl.ds(base, GRID_W), :]
                acc = acc + w_ref[j:j + 1, lanes] * src
            conv_ref[out_rows, lanes] = acc + cb_ref[:, lanes]
            return carry2

        lax.fori_loop(0, grid_h, grid_row, 0)
        for r in range(grid_h):
            acc = jnp.zeros((GRID_W, LANES), F32)
            for src in range(grid_h):
                tap = src + half - r
                acc = acc + (w_ref[tap:tap + 1, lanes_v]
                             * ver_ref[src * GRID_W:(src + 1) * GRID_W, lanes])
            conv_ref[r * GRID_W:(r + 1) * GRID_W, lanes_v] = acc + cb_ref[:, lanes_v]
        return carry

    lax.fori_loop(0, hc // LANES, lane_block, 0)
    _ln_swish_gate(conv_ref, az_ref, g_ref, b_ref, o_ref, n_tok)


def _conv_call(proj, conv_w, conv_b, ln_g, ln_b, batch, seq, col0, axial):
    taps, ch = conv_w.shape
    proj3 = proj.reshape(batch, seq, proj.shape[1])
    row = lambda v: v.reshape(1, ch)
    if axial:
        grid_h = seq // GRID_W
        assert taps // 2 >= grid_h - 1, "every grid row must lie inside the conv window"
        kern = functools.partial(_conv_axial_kernel, grid_h=grid_h, taps=taps)
        scratch = [pltpu.VMEM((grid_h, GRID_W + 2 * CONV_LEAD, ch // 2), F32),
                   pltpu.VMEM((SUBLANES - 1, GRID_W + 2 * CONV_LEAD - SUBLANES, LANES), F32),
                   pltpu.VMEM((seq, ch // 2), F32),
                   pltpu.VMEM((seq, ch), F32)]
    else:
        kern = functools.partial(_conv_seq_kernel, seq=seq, taps=taps)
        scratch = [pltpu.VMEM((seq + 2 * CONV_LEAD, ch), F32),
                   pltpu.VMEM((SUBLANES - 1, seq + 2 * CONV_LEAD - SUBLANES, LANES), F32),
                   pltpu.VMEM((seq, ch), F32)]
    assert taps // 2 < CONV_LEAD
    tok = lambda k: pl.BlockSpec((None, seq, ch), lambda b: (b, 0, col0 + k))
    const = lambda shape: pl.BlockSpec(shape, lambda b: (0, 0))
    out = pl.pallas_call(
        kern,
        grid=(batch,),
        in_specs=[tok(0), tok(1), tok(2), const((taps, ch)), const((1, ch)), const((1, ch)),
                  const((1, ch))],
        out_specs=pl.BlockSpec((None, seq, ch), lambda b: (b, 0, 0)),
        out_shape=jax.ShapeDtypeStruct((batch, seq, ch), BF16),
        scratch_shapes=scratch,
        compiler_params=_params(("parallel",)),
        name="conv_axial" if axial else "conv_seq",
    )(proj3, proj3, proj3, conv_w, row(conv_b), row(ln_g), row(ln_b))
    return out.reshape(batch * seq, ch)


def _split_bf16(x):
    hi = x.astype(BF16)
    return hi, (x - hi.astype(F32)).astype(BF16)


def _dot_split(a_parts, b_parts):
    (a_hi, a_lo), (b_hi, b_lo) = a_parts, b_parts
    n = b_hi.shape[1]
    rhs = jnp.concatenate([jnp.concatenate([b_hi, b_lo], axis=1),
                           jnp.concatenate([b_hi, jnp.zeros_like(b_lo)], axis=1)], axis=0)
    both = jnp.dot(jnp.concatenate([a_hi, a_lo], axis=1), rhs, preferred_element_type=F32)
    return both[:, 0:n] + both[:, n:2 * n]


def _block_diag(parts, keep_left, keep_right):
    return tuple(jnp.concatenate([x * keep_left, x * keep_right], axis=0) for x in parts)


def _unit_triangular_inverse_pairs(n2s, eye2, keep_left, keep_right):
    ps = [eye2 - n2 for n2 in n2s]
    n_parts = [_split_bf16(n2) for n2 in n2s]
    ss = [_dot_split(parts, _block_diag(parts, keep_left, keep_right)) for parts in n_parts]
    power = 2
    while 2 * power < CHUNK:
        s_parts = [_split_bf16(s) for s in ss]
        p_parts = [_split_bf16(p) for p in ps]
        prods = [_dot_split(tuple(jnp.concatenate([pp, sp], axis=0) for pp, sp in zip(pp2, sp2)),
                            _block_diag(sp2, keep_left, keep_right))
                 for pp2, sp2 in zip(p_parts, s_parts)]
        ps = [p + pr[0:CHUNK] for p, pr in zip(ps, prods)]
        ss = [pr[CHUNK:2 * CHUNK] for pr in prods]
        power *= 2
    last = [_dot_split(_split_bf16(p), _block_diag(_split_bf16(s), keep_left, keep_right))
            for p, s in zip(ps, ss)]
    return [p + pr for p, pr in zip(ps, last)]


def _delta_kernel(*refs, seq, hb_count, chunk_unroll, zero_init, n_carried):
    (q_ref, k_ref, v_ref, z_ref, ba_ref, bat_ref, wq_ref, wk_ref, wv_ref, hng_ref) = refs[:10]
    rest = list(refs[10:])
    s0_ref = None if zero_init else rest.pop(0)
    carry_ref = rest.pop(0) if n_carried else None
    (y_ref, sout_ref, qs, ks, vs, gates, u_scr, wq_scr, lx_scr, egl_scr, o_scr, sscr) = rest
    nc = seq // CHUNK
    wq_rows = 2 * CHUNK
    lx_rows = CHUNK + HEAD_DIM
    h0 = pl.program_id(1) * hb_count

    row8 = lax.broadcasted_iota(jnp.int32, (SUBLANES, HEAD_DIM), 0)
    lane = lax.broadcasted_iota(jnp.int32, (1, LANES), 1)

    def conv_swish(x_ref, w_ref, cols):
        x = x_ref[:, cols].astype(F32)
        prev = pltpu.roll(x, 1, 0)
        prev = jnp.concatenate([jnp.where(row8 >= 1, prev[0:SUBLANES], 0.0), prev[SUBLANES:]], axis=0)
        nxt = pltpu.roll(x, seq - 1, 0)
        nxt = jnp.concatenate([nxt[0:seq - SUBLANES],
                               jnp.where(row8 < SUBLANES - 1, nxt[seq - SUBLANES:], 0.0)], axis=0)
        return _silu(w_ref[0:1, cols] * prev + w_ref[1:2, cols] * x + w_ref[2:3, cols] * nxt)

    def l2norm(x):
        return x * lax.rsqrt(jnp.sum(x * x, axis=-1, keepdims=True) + EPS)

    ba = ba_ref[...]
    for hb in range(hb_count):
        cols = slice(hb * HEAD_DIM, (hb + 1) * HEAD_DIM)
        qs[hb] = l2norm(conv_swish(q_ref, wq_ref, cols)) * (HEAD_DIM ** -0.5)
        ks[hb] = l2norm(conv_swish(k_ref, wk_ref, cols))
        vs[hb] = conv_swish(v_ref, wv_ref, cols)
        for t in range(4):
            col = jnp.sum(jnp.where(lane == t * N_HEADS + h0 + hb, ba, 0.0), axis=-1, keepdims=True)
            gates[t, hb] = jnp.broadcast_to(col, (seq, LANES))
        for d in range(2):
            if zero_init:
                sscr[d, hb] = jnp.zeros((HEAD_DIM, HEAD_DIM), F32)
            else:
                sscr[d, hb] = s0_ref[d, hb]

    ii = lax.broadcasted_iota(jnp.int32, (CHUNK, LANES), 0)
    l64 = lax.broadcasted_iota(jnp.int32, (CHUNK, LANES), 1)
    left = l64 < CHUNK
    right = l64 >= CHUNK
    jj = l64 & (CHUNK - 1)
    below = jnp.where(left, ii - jj, jj - ii)
    incl = below >= 0
    strict = below > 0
    eye2 = jnp.where(ii == jj, 1.0, 0.0).astype(F32)
    keep_left = jnp.where(left, 1.0, 0.0).astype(BF16)
    keep_right = jnp.where(left, 0.0, 1.0).astype(BF16)
    lane_t = lax.broadcasted_iota(jnp.int32, (HEAD_DIM, LANES), 1)
    left_t = lane_t < CHUNK
    right_t = lane_t >= CHUNK
    zeros = jnp.zeros((CHUNK, HEAD_DIM), F32)

    def chunk_products(problems):
        kqs = [_dot_nt(jnp.concatenate([k, q], axis=0), jnp.concatenate([k, k], axis=0))
               for q, k, *_ in problems]
        decays = []
        n2s = []
        for (q, k, v, bf, bb, gf, gb, grow2), kq in zip(problems, kqs):
            diff = jnp.where(left, gf, gb) - grow2
            decay = jnp.where(incl, jnp.exp(jnp.where(incl, diff, 0.0)), 0.0)
            decays.append(decay)
            n2s.append(jnp.where(strict, jnp.where(left, bf, bb) * kq[0:CHUNK] * decay, 0.0))
        tinvs = _unit_triangular_inverse_pairs(n2s, eye2, keep_left, keep_right)
        rhss = []
        for q, k, v, bf, bb, gf, gb, grow2 in problems:
            rhss.append(jnp.concatenate(
                [jnp.concatenate([v * bf, k * bf * jnp.exp(gf), zeros, zeros], axis=1),
                 jnp.concatenate([zeros, zeros, v * bb, k * bb * jnp.exp(gb)], axis=1)], axis=0))
        uws = [_dot(tinv, rhs) for tinv, rhs in zip(tinvs, rhss)]
        results = []
        for (q, k, v, bf, bb, gf, gb, grow2), kq, decay, uw in zip(problems, kqs, decays, uws):
            glf = gf[CHUNK - 1:CHUNK, :]
            glb = gb[0:1, :]
            kt2 = jnp.concatenate([k * jnp.exp(glf - gf), k * jnp.exp(glb - gb)], axis=0).T
            qkm = kq[CHUNK:2 * CHUNK] * decay
            out = []
            for d, (g, gl) in enumerate(((gf, glf), (gb, glb))):
                keep = left if d == 0 else right
                keep_t = left_t if d == 0 else right_t
                u = uw[:, (2 * d) * HEAD_DIM:(2 * d + 1) * HEAD_DIM]
                w = uw[:, (2 * d + 1) * HEAD_DIM:(2 * d + 2) * HEAD_DIM]
                wq = jnp.concatenate([w, q * jnp.exp(g)], axis=0).astype(BF16)
                lx = jnp.concatenate([jnp.where(keep, qkm, 0.0), jnp.where(keep_t, kt2, 0.0)],
                                     axis=0).astype(BF16)
                out.append((u, wq, lx, jnp.broadcast_to(jnp.exp(gl), (8, LANES))))
            results.append(out)
        return results

    def phase1(i, carry):
        where = []
        problems = []
        for j in range(chunk_unroll):
            c = i * chunk_unroll + j
            rows = pl.ds(pl.multiple_of(c * CHUNK, CHUNK), CHUNK)
            for hb in range(hb_count):
                where.append((c, rows, hb))
                problems.append((qs[hb, rows, :], ks[hb, rows, :], vs[hb, rows, :],
                                 gates[0, hb, rows, :], gates[1, hb, rows, :],
                                 gates[2, hb, rows, :], gates[3, hb, rows, :],
                                 bat_ref[c, h0 + hb]))
        for (c, rows, hb), res in zip(where, chunk_products(problems)):
            for d, (u, wq, lx, egl) in enumerate(res):
                u_scr[d, hb, rows, :] = u
                wq_scr[d, hb, pl.ds(pl.multiple_of(c * wq_rows, wq_rows), wq_rows), :] = wq
                lx_scr[d, hb, pl.ds(pl.multiple_of(c * lx_rows, CHUNK), lx_rows), :] = lx
                egl_scr[d, hb, c] = egl
        return carry

    lax.fori_loop(0, nc // chunk_unroll, phase1, 0)

    def phase2(c, carry):
        work = []
        for hb in range(hb_count):
            for d in range(2):
                cd = c if d == 0 else nc - 1 - c
                rows = pl.ds(pl.multiple_of(cd * CHUNK, CHUNK), CHUNK)
                work.append((hb, d, rows, sscr[d, hb], u_scr[d, hb, rows, :],
                             wq_scr[d, hb, pl.ds(pl.multiple_of(cd * wq_rows, wq_rows), wq_rows), :],
                             lx_scr[d, hb, pl.ds(pl.multiple_of(cd * lx_rows, CHUNK), lx_rows), :],
                             egl_scr[d, hb, cd]))
        wss = [jnp.dot(wq, state.astype(BF16), preferred_element_type=F32)
               for _, _, _, state, _, wq, _, _ in work]
        v_news = [(u - ws[0:CHUNK]).astype(BF16)
                  for (_, _, _, _, u, _, _, _), ws in zip(work, wss)]
        oms = [jnp.dot(lx, jnp.concatenate([v_new, v_new], axis=0), preferred_element_type=F32)
               for (_, _, _, _, _, _, lx, _), v_new in zip(work, v_news)]
        for (hb, d, rows, state, _, _, _, egl), ws, om in zip(work, wss, oms):
            o_scr[d, hb, rows, :] = ws[CHUNK:2 * CHUNK] + om[0:CHUNK]
            sscr[d, hb] = state * egl[0:1, :] + om[CHUNK:lx_rows]
        return carry

    lax.fori_loop(0, nc, phase2, 0)

    for hb in range(hb_count):
        cols = slice(hb * HEAD_DIM, (hb + 1) * HEAD_DIM)
        o = o_scr[0, hb] + o_scr[1, hb]
        o = o * lax.rsqrt(jnp.mean(o * o, axis=-1, keepdims=True) + EPS) * hng_ref[...]
        y_ref[:, cols] = (o * _silu(z_ref[:, cols].astype(F32))).astype(y_ref.dtype)
        for d in range(2):
            sout_ref[n_carried, d, hb] = sscr[d, hb]
    for p in range(n_carried):
        sout_ref[p] = carry_ref[p]


def _delta_call(proj, ba, bat, conv_qkv_w, head_norm_g, s0, carried, batch, seq, col_q):
    nc = seq // CHUNK
    hb_count = N_HEADS if seq <= 256 else N_HEADS // 2
    chunk_unroll = PHASE1_CHUNKS
    n_hblk = N_HEADS // hb_count
    bw = hb_count * HEAD_DIM
    kd = N_HEADS * HEAD_DIM
    proj3 = proj.reshape(batch, seq, proj.shape[1])
    ba3 = ba.reshape(batch, seq, LANES)
    cq = col_q // hb_count
    tok = lambda k: pl.BlockSpec((None, seq, bw), lambda b, h: (b, 0, cq + k * n_hblk + h))
    cw = lambda k: pl.BlockSpec((3, bw), lambda b, h: (0, k * n_hblk + h))
    in_specs = [tok(0), tok(1), tok(2), tok(3),
                pl.BlockSpec((None, seq, LANES), lambda b, h: (b, 0, 0)),
                pl.BlockSpec((nc, N_HEADS, 1, LANES), lambda b, h: (b, 0, 0, 0)),
                cw(0), cw(1), cw(2),
                pl.BlockSpec((1, HEAD_DIM), lambda b, h: (0, 0))]
    args = [proj3, proj3, proj3, proj3, ba3, bat, conv_qkv_w, conv_qkv_w, conv_qkv_w,
            head_norm_g.reshape(1, HEAD_DIM)]
    layers_spec = lambda n: pl.BlockSpec((None, n, 2, hb_count, HEAD_DIM, HEAD_DIM),
                                         lambda b, h: (b, 0, 0, h, 0, 0))
    if s0 is not None:
        states, layer = s0
        in_specs.append(pl.BlockSpec((None, None, 2, hb_count, HEAD_DIM, HEAD_DIM),
                                     lambda b, h: (b, layer, 0, h, 0, 0)))
        args.append(states)
    n_carried = 0 if carried is None else carried.shape[1]
    if n_carried:
        in_specs.append(layers_spec(n_carried))
        args.append(carried)
    head_scr = pltpu.VMEM((hb_count, seq, HEAD_DIM), F32)
    dir_scr = pltpu.VMEM((2, hb_count, seq, HEAD_DIM), F32)
    y, s_out = pl.pallas_call(
        functools.partial(_delta_kernel, seq=seq, hb_count=hb_count, chunk_unroll=chunk_unroll,
                          zero_init=s0 is None, n_carried=n_carried),
        grid=(batch, n_hblk),
        in_specs=in_specs,
        out_specs=[pl.BlockSpec((None, seq, bw), lambda b, h: (b, 0, h)),
                   layers_spec(n_carried + 1)],
        out_shape=[jax.ShapeDtypeStruct((batch, seq, kd), BF16),
                   jax.ShapeDtypeStruct((batch, n_carried + 1, 2, N_HEADS, HEAD_DIM, HEAD_DIM),
                                        F32)],
        scratch_shapes=[head_scr, head_scr, head_scr,
                        pltpu.VMEM((4, hb_count, seq, LANES), F32),
                        dir_scr,
                        pltpu.VMEM((2, hb_count, nc * 2 * CHUNK, HEAD_DIM), BF16),
                        pltpu.VMEM((2, hb_count, nc * (CHUNK + HEAD_DIM), LANES), BF16),
                        pltpu.VMEM((2, hb_count, nc, 8, LANES), F32),
                        dir_scr,
                        pltpu.VMEM((2, hb_count, HEAD_DIM, HEAD_DIM), F32)],
        compiler_params=_params(("parallel", "parallel")),
        name="delta",
    )(*args)
    return y.reshape(batch * seq, kd), s_out


def _out_proj_kernel(x_ref, ya_ref, yb_ref, ga_ref, gb_ref, mod_ref, wpa_ref, wpb_ref, wo_ref,
                     fg_ref, o_ref, *, d, final):
    gate = mod_ref[:, 2 * d:3 * d]
    out_a = jnp.dot(ya_ref[...], wpa_ref[...], preferred_element_type=F32)
    out_b = jnp.dot(yb_ref[...], wpb_ref[...], preferred_element_type=F32)
    merged = (_sigmoid(ga_ref[...].astype(F32)) * out_a
              + _sigmoid(gb_ref[...].astype(F32)) * out_b)
    x = x_ref[...] + gate * jnp.dot(merged.astype(BF16), wo_ref[...], preferred_element_type=F32)
    if final:
        x = x * lax.rsqrt(jnp.mean(x * x, axis=-1, keepdims=True) + EPS) * fg_ref[...]
    o_ref[...] = x


def _out_proj_call(x, ya, yb, proj, mod, w_pa, w_pb, w_o, final_g, layer, tokens_per_mod, final):
    n, d = x.shape
    cc = ya.shape[1]
    tm = 512
    kern = functools.partial(_out_proj_kernel, d=d, final=final)
    resident = lambda shape: pl.BlockSpec((None,) + tuple(shape[1:]), lambda i: (layer, 0, 0),
                                          pipeline_mode=pl.Buffered(1))
    return pl.pallas_call(
        kern,
        grid=(n // tm,),
        in_specs=[pl.BlockSpec((tm, d), lambda i: (i, 0)),
                  pl.BlockSpec((tm, cc), lambda i: (i, 0)),
                  pl.BlockSpec((tm, cc), lambda i: (i, 0)),
                  pl.BlockSpec((tm, d), lambda i: (i, 0)),
                  pl.BlockSpec((tm, d), lambda i: (i, 1)),
                  pl.BlockSpec((None, 1, 3 * d), lambda i: (_mod_row(i, tm, tokens_per_mod), 0, 0)),
                  resident(w_pa.shape), resident(w_pb.shape), resident(w_o.shape),
                  pl.BlockSpec((1, d), lambda i: (0, 0))],
        out_specs=pl.BlockSpec((tm, d), lambda i: (i, 0)),
        out_shape=jax.ShapeDtypeStruct((n, d), F32),
        compiler_params=_params(("parallel",)),
        name="out_proj",
    )(x, ya, yb, proj, proj, mod, w_pa, w_pb, w_o, final_g)


def kernel(x_prompt, x_sample, state_delta, c, c_ctx, w_mod, b_mod, norm_g, w_in, conv_a_w,
           conv_a_b, ln_a_g, ln_a_b, w_pa, conv_qkv_w, a_log, dt_bias, head_norm_g, w_pb, w_o,
           final_norm_g):
    batch, seq, d = x_prompt.shape
    dec_batch, dec_seq, _ = x_sample.shape
    depth = w_mod.shape[0]
    cc = conv_a_w.shape[2]
    kd = N_HEADS * HEAD_DIM
    o_ba = 3 * cc + 4 * kd
    o_gate = o_ba + 4 * N_HEADS

    cv = jnp.concatenate([c_ctx[None, :], c, jnp.zeros((MOD_ROWS - 1 - dec_batch, d), F32)], axis=0)
    mod2 = _mod_call(cv, w_mod, b_mod)
    mod = mod2.reshape(depth, MOD_ROWS, 1, 3 * d)

    w_t = jnp.swapaxes(w_in, 1, 2)
    gate_pad = ((0, 0), (2 * N_HEADS, LANES - 4 * N_HEADS))
    alog_rows = jnp.pad(a_log.reshape(depth, 2 * N_HEADS), gate_pad).reshape(depth, 1, LANES)
    dtb_rows = jnp.pad(dt_bias.reshape(depth, 2 * N_HEADS), gate_pad).reshape(depth, 1, LANES)
    w_pa16, w_pb16, w_o16 = w_pa.astype(BF16), w_pb.astype(BF16), w_o.astype(BF16)
    col_conv = (2 * d) // cc
    col_q = (2 * d + 3 * cc) // HEAD_DIM
    final_g = final_norm_g.reshape(1, d)

    def layer(x, l, latent, carried):
        b, s = (dec_batch, dec_seq) if latent else (batch, seq)
        tokens_per_mod = dec_seq if latent else None
        proj, ba, bat = _in_proj_call(x, mod2[l], norm_g[l].reshape(1, d), w_t, alog_rows[l],
                                      dtb_rows[l], l, o_ba, o_gate - o_ba, tokens_per_mod)
        ya = _conv_call(proj, conv_a_w[l], conv_a_b[l], ln_a_g[l], ln_a_b[l], b, s, col_conv,
                        axial=latent)
        s0 = (state_delta, l) if latent else None
        yb, states = _delta_call(proj, ba, bat, conv_qkv_w[l], head_norm_g[l], s0, carried, b, s,
                                 col_q)
        x = _out_proj_call(x, ya, yb, proj, mod[l], w_pa16, w_pb16, w_o16, final_g, l,
                           tokens_per_mod, final=(l == depth - 1))
        return x, states

    h = x_prompt.reshape(batch * seq, d)
    hs = x_sample.reshape(dec_batch * dec_seq, d)
    ctx_states = None
    for l in range(depth):
        h, ctx_states = layer(h, l, latent=False, carried=ctx_states)
        hs, _ = layer(hs, l, latent=True, carried=None)
    y_prompt = h.reshape(batch, seq, d)
    y_sample = hs.reshape(dec_batch, dec_seq, d)
    return (y_prompt, y_sample, ctx_states)
```
